```python
import jax, jax.numpy as jnp
from jax import lax
import numpy as np

D_MODEL = 2048
BATCH = 4
SEQ = 4096
DEPTH = 2

N_META = 16
BLOCK = 128
PAD = BLOCK - N_META
EPS = 1e-6
MASK_VALUE = -1e30

FOX_HEADS = 8
FOX_HEAD_DIM = D_MODEL // 16
FOX_WIDTH = FOX_HEADS * FOX_HEAD_DIM

GLA_HEADS = 4
GLA_KEY_WIDTH = D_MODEL // 2
GLA_VAL_WIDTH = D_MODEL
GLA_DK = GLA_KEY_WIDTH // GLA_HEADS
GLA_DV = GLA_VAL_WIDTH // GLA_HEADS
GLA_RANK = 16
GLA_TAU = 16.0
GLA_CHUNK = 64

D_FF = 4 * D_MODEL

SPLITS = (FOX_WIDTH, FOX_WIDTH, FOX_WIDTH, FOX_HEADS,
          GLA_KEY_WIDTH, GLA_KEY_WIDTH, GLA_VAL_WIDTH, GLA_VAL_WIDTH, GLA_RANK,
          D_MODEL, D_MODEL)
D_IN_PROJ = 3 * FOX_WIDTH + FOX_HEADS + 2 * GLA_KEY_WIDTH + 2 * GLA_VAL_WIDTH + GLA_RANK + 2 * D_MODEL

kernel_name = "fox_gla_gated_hybrid_block"


def rmsnorm(x, g):
    xf = x.astype(jnp.float32)
    y = xf * lax.rsqrt(jnp.mean(xf * xf, axis=-1, keepdims=True) + EPS)
    return (y * g.astype(jnp.float32)).astype(x.dtype)


def split_heads(t, h):
    b, l, _ = t.shape
    return t.reshape(b, l, h, -1).transpose(0, 2, 1, 3)


def forgetting_attention(q, k, v, f_logit, b_forget):
    B, L, _ = q.shape
    Lp = L + PAD
    pad4 = ((0, 0), (0, 0), (PAD, 0), (0, 0))
    qh = jnp.pad(split_heads(q, FOX_HEADS), pad4)
    kh = jnp.pad(split_heads(k, FOX_HEADS), pad4)
    vh = jnp.pad(split_heads(v, FOX_HEADS), pad4)
    log_f = jax.nn.log_sigmoid((f_logit + b_forget).astype(jnp.float32))
    log_f = jnp.pad(log_f.transpose(0, 2, 1), ((0, 0), (0, 0), (PAD, 0)))
    c = jnp.cumsum(log_f, axis=-1)
    scale = FOX_HEAD_DIM ** -0.5
    pos = np.arange(Lp)
    outs = []
    for i in range(Lp // BLOCK):
        q0, q1 = i * BLOCK, (i + 1) * BLOCK
        s = jnp.einsum('bhqd,bhkd->bhqk', qh[:, :, q0:q1], kh[:, :, :q1],
                       preferred_element_type=jnp.float32) * scale
        s = s + c[:, :, q0:q1, None] - c[:, :, None, :q1]
        mask = (pos[None, :q1] <= pos[q0:q1, None]) & (pos[None, :q1] >= PAD)
        s = jnp.where(mask, s, MASK_VALUE)
        p = jax.nn.softmax(s, axis=-1)
        outs.append(jnp.einsum('bhqk,bhkd->bhqd', p.astype(vh.dtype), vh[:, :, :q1]))
    o = jnp.concatenate(outs, axis=2)[:, :, PAD:]
    return o.transpose(0, 2, 1, 3).reshape(B, L, FOX_WIDTH)


def gated_linear_attention(q, k, v, g):
    B, L, _ = q.shape
    Lp = L + PAD
    N, C = Lp // GLA_CHUNK, GLA_CHUNK

    def chunks(t):
        t = jnp.pad(t.astype(jnp.float32), ((0, 0), (PAD, 0), (0, 0)))
        return t.reshape(B, N, C, GLA_HEADS, -1).transpose(0, 3, 1, 2, 4)

    qc = chunks(q) * (GLA_DK ** -0.5)
    kc, vc, gc = chunks(k), chunks(v), chunks(g)
    b = jnp.cumsum(gc, axis=3)
    b_last = b[:, :, :, -1:]
    q_dec = qc * jnp.exp(b)
    k_inv = kc * jnp.exp(-b)
    k_end = kc * jnp.exp(b_last - b)
    causal = np.tril(np.ones((C, C), dtype=bool))
    a = jnp.where(causal, jnp.einsum('bhnck,bhnsk->bhncs', q_dec, k_inv), 0.0)
    o_intra = jnp.einsum('bhncs,bhnsv->bhncv', a, vc)

    def step(S, inp):
        qd, ke, vv, dl = inp
        o = jnp.einsum('bhck,bhkv->bhcv', qd, S)
        S = S * dl[..., None] + jnp.einsum('bhck,bhcv->bhkv', ke, vv)
        return S, o

    S0 = jnp.zeros((B, GLA_HEADS, GLA_DK, GLA_DV), jnp.float32)
    xs = (jnp.moveaxis(q_dec, 2, 0), jnp.moveaxis(k_end, 2, 0), jnp.moveaxis(vc, 2, 0),
          jnp.moveaxis(jnp.exp(b_last[:, :, :, 0]), 2, 0))
    _, o_inter = lax.scan(step, S0, xs)
    o = o_intra + jnp.moveaxis(o_inter, 0, 2)
    return o.transpose(0, 2, 3, 1, 4).reshape(B, Lp, GLA_VAL_WIDTH)[:, PAD:]


def head_rmsnorm(o, g):
    B, L, _ = o.shape
    oh = o.reshape(B, L, GLA_HEADS, GLA_DV)
    oh = oh * lax.rsqrt(jnp.mean(oh * oh, axis=-1, keepdims=True) + EPS)
    return oh.reshape(B, L, GLA_VAL_WIDTH) * g.astype(jnp.float32)


def hybrid_mixer(xn, w_in, b_forget, w_alpha2, b_alpha, gla_norm_g, w_o_fox, w_o_gla, w_out):
    proj = xn @ w_in
    split_points = np.cumsum(SPLITS)[:-1].tolist()
    (fq, fk, fv, ff, gq, gk, gv, gr, ga, gate_fox, gate_gla) = jnp.split(proj, split_points, axis=-1)
    o_fox = forgetting_attention(fq, fk, fv, ff, b_forget)
    g_log = jax.nn.log_sigmoid((ga @ w_alpha2 + b_alpha).astype(jnp.float32)) / GLA_TAU
    o_gla = gated_linear_attention(gq, gk, gv, g_log)
    o_gla = (head_rmsnorm(o_gla, gla_norm_g) * jax.nn.silu(gr.astype(jnp.float32))).astype(xn.dtype)
    y = jax.nn.sigmoid(gate_fox) * (o_fox @ w_o_fox) + jax.nn.sigmoid(gate_gla) * (o_gla @ w_o_gla)
    return y @ w_out


def squared_relu_mlp(xn, w1, w2):
    return jnp.square(jax.nn.relu(xn @ w1)) @ w2


def setup_inputs(seed: int = 0) -> dict:
    key = jax.random.key(seed)
    ks = jax.random.split(key, 16)
    nrm = lambda k, shape, fan_in: jax.random.normal(k, shape, jnp.float32) * (fan_in ** -0.5)
    gain = lambda k, shape: 1.0 + 0.02 * jax.random.normal(k, shape, jnp.float32)
    return {
        "x": jax.random.normal(ks[0], (BATCH, SEQ, D_MODEL), jnp.float32),
        "meta_tokens": jax.random.normal(ks[1], (N_META, D_MODEL), jnp.float32),
        "norm_mix_g": gain(ks[2], (DEPTH, D_MODEL)),
        "w_in": nrm(ks[3], (DEPTH, D_MODEL, D_IN_PROJ), D_MODEL),
        "b_forget": jax.random.uniform(ks[4], (DEPTH, FOX_HEADS), jnp.float32, 1.0, 5.0),
        "w_alpha2": nrm(ks[5], (DEPTH, GLA_RANK, GLA_KEY_WIDTH), GLA_RANK),
        "b_alpha": 0.01 * jax.random.normal(ks[6], (DEPTH, GLA_KEY_WIDTH), jnp.float32),
        "gla_norm_g": gain(ks[7], (DEPTH, GLA_VAL_WIDTH)),
        "w_o_fox": nrm(ks[8], (DEPTH, FOX_WIDTH, D_MODEL), FOX_WIDTH),
        "w_o_gla": nrm(ks[9], (DEPTH, GLA_VAL_WIDTH, D_MODEL), GLA_VAL_WIDTH),
        "w_out": nrm(ks[10], (DEPTH, D_MODEL, D_MODEL), D_MODEL),
        "norm_mlp_g": gain(ks[11], (DEPTH, D_MODEL)),
        "w_ff1": nrm(ks[12], (DEPTH, D_MODEL, D_FF), D_MODEL),
        "w_ff2": nrm(ks[13], (DEPTH, D_FF, D_MODEL), D_FF),
        "final_norm_g": gain(ks[14], (D_MODEL,)),
    }


def reference(x, meta_tokens, norm_mix_g, w_in, b_forget, w_alpha2, b_alpha, gla_norm_g,
              w_o_fox, w_o_gla, w_out, norm_mlp_g, w_ff1, w_ff2, final_norm_g):
    B = x.shape[0]
    meta = jnp.broadcast_to(meta_tokens[None].astype(x.dtype), (B, N_META, D_MODEL))
    h = jnp.concatenate([meta, x], axis=1)
    for l in range(DEPTH):
        h = h + hybrid_mixer(rmsnorm(h, norm_mix_g[l]), w_in[l], b_forget[l], w_alpha2[l],
                             b_alpha[l], gla_norm_g[l], w_o_fox[l], w_o_gla[l], w_out[l])
        h = h + squared_relu_mlp(rmsnorm(h, norm_mlp_g[l]), w_ff1[l], w_ff2[l])
    return rmsnorm(h, final_norm_g)[:, N_META:]
```

```python
import functools

import jax
import jax.numpy as jnp
from jax import lax
from jax.experimental import pallas as pl
from jax.experimental.pallas import tpu as pltpu

F32 = jnp.float32
BF16 = jnp.bfloat16

D_MODEL = 2048
BATCH = 4
SEQ = 4096
DEPTH = 2
N_META = 16
EPS = 1e-6
MASK_VALUE = -1e30

FOX_HEADS = 8
FOX_HEAD_DIM = 128
FOX_WIDTH = FOX_HEADS * FOX_HEAD_DIM
GLA_HEADS = 4
GLA_DK = 256
GLA_DV = 512
GLA_KEY_WIDTH = GLA_HEADS * GLA_DK
GLA_VAL_WIDTH = GLA_HEADS * GLA_DV
GLA_RANK = 16
GLA_TAU = 16.0
GLA_CHUNK = 64
D_FF = 4 * D_MODEL

LANES = 128
V7X_VMEM_LIMIT_CAP = 60 * 1024 * 1024

T_ROWS = N_META + SEQ + (LANES - N_META)
M_ROWS = BATCH * T_ROWS
T_LANE_ROWS = T_ROWS // LANES
CUM_ROWS = 64

COL_GV = 0
COL_GR = COL_GV + GLA_VAL_WIDTH
COL_GATE_FOX = COL_GR + GLA_VAL_WIDTH
COL_GATE_GLA = COL_GATE_FOX + D_MODEL
COL_FQ = COL_GATE_GLA + D_MODEL
COL_FK = COL_FQ + FOX_WIDTH
COL_FV = COL_FK + FOX_WIDTH
COL_GQ = COL_FV + FOX_WIDTH
COL_GK = COL_GQ + GLA_KEY_WIDTH
N_PROJ = COL_GK + GLA_KEY_WIDTH
SMALL_W = LANES
SMALL_FF0 = GLA_RANK

NORM_CHUNK = 128
INPROJ_BM, INPROJ_BN = 1408, 1024
MERGE_BM, MERGE_BN = 1408, 512
OUTPROJ_BM, OUTPROJ_BN = 1408, 512
MLP_BM, MLP_BF = 768, 512
FOX_BQ = 384
FOX_BK = 384
GLA_RB = 384


def _vmem_limit(nbytes):
    return int(min(V7X_VMEM_LIMIT_CAP, nbytes + (4 << 20)))


def _params(sem, nbytes):
    return pltpu.CompilerParams(dimension_semantics=sem, vmem_limit_bytes=_vmem_limit(nbytes))


def _rmsnorm_rows(src_ref, g_ref, dst_ref, rows):
    g = g_ref[...]

    def body(c, _):
        r0 = pl.multiple_of(c * NORM_CHUNK, NORM_CHUNK)
        x = src_ref[pl.ds(r0, NORM_CHUNK), :]
        ms = jnp.mean(x * x, axis=-1, keepdims=True)
        dst_ref[pl.ds(r0, NORM_CHUNK), :] = ((x * lax.rsqrt(ms + EPS)) * g).astype(dst_ref.dtype)
        return 0

    lax.fori_loop(0, rows // NORM_CHUNK, body, 0)


def _log_sigmoid(x):
    return jnp.minimum(x, 0.0) - jnp.log1p(jnp.exp(-jnp.abs(x)))


def _sigmoid(x):
    return 1.0 / (1.0 + jnp.exp(-x))


def _dot(a, b):
    return jnp.dot(a, b, preferred_element_type=F32)


def _dot_nt(a, b):
    return lax.dot_general(a, b, (((1,), (1,)), ((), ())), preferred_element_type=F32)


def _dot_tn(a, b):
    return lax.dot_general(a, b, (((0,), (0,)), ((), ())), preferred_element_type=F32)


def _split_bf16(x, n):
    pieces = []
    rem = x
    for _ in range(n):
        hi = rem.astype(BF16)
        pieces.append(hi)
        rem = rem - hi.astype(F32)
    return pieces


def _dot_f32_lhs(x, r, n):
    acc = None
    for p in _split_bf16(x, n):
        t = _dot(p, r)
        acc = t if acc is None else acc + t
    return acc


def _dot_f32_rhs(l, x, n):
    acc = None
    for p in _split_bf16(x, n):
        t = _dot(l, p)
        acc = t if acc is None else acc + t
    return acc


def _inproj_kernel(h_ref, g_ref, w_ref, ws_ref, p_ref, small_ref, fft_ref, xn_ref):
    @pl.when(pl.program_id(1) == 0)
    def _():
        _rmsnorm_rows(h_ref, g_ref, xn_ref, INPROJ_BM)
        sm = _dot(xn_ref[...], ws_ref[...])
        small_ref[...] = sm
        fft_ref[...] = sm.T[SMALL_FF0:SMALL_FF0 + FOX_HEADS, :]

    p_ref[...] = _dot(xn_ref[...], w_ref[...]).astype(BF16)


def _inproj(h, g, w_main, w_small):
    bm, bn = INPROJ_BM, INPROJ_BN
    nbytes = (2 * bm * D_MODEL * 4 + 2 * D_MODEL * bn * 2 + 2 * D_MODEL * SMALL_W * 2
              + 2 * bm * bn * 2 + 2 * bm * SMALL_W * 4 + bm * D_MODEL * 2 + bm * bn * 4 + 2 * 8 * bm * 4)
    return pl.pallas_call(
        _inproj_kernel,
        grid=(M_ROWS // bm, N_PROJ // bn),
        in_specs=[
            pl.BlockSpec((bm, D_MODEL), lambda i, j: (i, 0)),
            pl.BlockSpec((1, D_MODEL), lambda i, j: (0, 0)),
            pl.BlockSpec((D_MODEL, bn), lambda i, j: (0, j)),
            pl.BlockSpec((D_MODEL, SMALL_W), lambda i, j: (0, 0)),
        ],
        out_specs=[
            pl.BlockSpec((bm, bn), lambda i, j: (i, j)),
            pl.BlockSpec((bm, SMALL_W), lambda i, j: (i, 0)),
            pl.BlockSpec((FOX_HEADS, bm), lambda i, j: (0, i)),
        ],
        out_shape=[
            jax.ShapeDtypeStruct((M_ROWS, N_PROJ), BF16),
            jax.ShapeDtypeStruct((M_ROWS, SMALL_W), F32),
            jax.ShapeDtypeStruct((FOX_HEADS, M_ROWS), F32),
        ],
        scratch_shapes=[pltpu.VMEM((bm, D_MODEL), BF16)],
        compiler_params=_params(("arbitrary", "arbitrary"), nbytes),
        name="inproj",
    )(h, g, w_main, w_small)


def _fox_cumsum_kernel(bf_ref, ff_ref, nc_ref):
    li = lax.broadcasted_iota(jnp.int32, (LANES, LANES), 0)
    lj = lax.broadcasted_iota(jnp.int32, (LANES, LANES), 1)
    upper = jnp.where(li <= lj, 1.0, 0.0).astype(BF16)
    ones = jnp.ones((LANES, LANES), BF16)
    ri = lax.broadcasted_iota(jnp.int32, (CUM_ROWS, CUM_ROWS), 0)
    rj = lax.broadcasted_iota(jnp.int32, (CUM_ROWS, CUM_ROWS), 1)
    strict_lower = jnp.where(rj < ri, 1.0, 0.0).astype(BF16)
    for hd in range(FOX_HEADS):
        lf = _log_sigmoid(ff_ref[hd, 0] + bf_ref[hd])
        c = _dot_f32_lhs(lf, upper, 3) + _dot_f32_rhs(strict_lower, _dot_f32_lhs(lf, ones, 3), 3)
        nc_ref[0, hd] = -c


def _fox_cumsum(b_forget, ff_rows):
    return pl.pallas_call(
        _fox_cumsum_kernel,
        grid=(BATCH,),
        in_specs=[
            pl.BlockSpec(memory_space=pltpu.SMEM),
            pl.BlockSpec((FOX_HEADS, 1, CUM_ROWS, LANES), lambda b: (0, b, 0, 0)),
        ],
        out_specs=pl.BlockSpec((1, FOX_HEADS, CUM_ROWS, LANES), lambda b: (b, 0, 0, 0)),
        out_shape=jax.ShapeDtypeStruct((BATCH, FOX_HEADS, CUM_ROWS, LANES), F32),
        compiler_params=_params(("arbitrary",), 8 << 20),
        name="fox_cumsum",
    )(b_forget, ff_rows)


def _fox_kernel(q_ref, k_ref, v_ref, nc_ref, o_ref, vx_ref):
    vx_ref[:, :FOX_HEAD_DIM] = v_ref[...]
    vx_ref[:, FOX_HEAD_DIM:] = jnp.ones((T_ROWS, FOX_HEAD_DIM), BF16)
    lane_chunks = FOX_BK // LANES
    row_id = lax.broadcasted_iota(jnp.int32, (FOX_BQ, FOX_BK), 0)
    col_id = lax.broadcasted_iota(jnp.int32, (FOX_BQ, FOX_BK), 1)
    causal = col_id <= row_id

    def q_body(qi, _):
        q0 = pl.multiple_of(qi * FOX_BQ, FOX_BQ)
        q = q_ref[pl.ds(q0, FOX_BQ), :]

        def scores(kj):
            k0 = pl.multiple_of(kj * FOX_BK, FOX_BK)
            s = _dot_nt(q, k_ref[pl.ds(k0, FOX_BK), :])
            bias = jnp.concatenate(
                [nc_ref[0, 0, pl.ds(kj * lane_chunks + c, 1), :] for c in range(lane_chunks)], axis=1)
            return s + bias, k0

        def update(s, k0, m, acc):
            m_new = jnp.maximum(m, jnp.max(s, axis=1, keepdims=True))
            alpha = jnp.exp(m - m_new)
            p = jnp.exp(s - m_new).astype(BF16)
            return m_new, alpha * acc + _dot(p, vx_ref[pl.ds(k0, FOX_BK), :])

        def kv_body(kj, carry):
            s, k0 = scores(kj)
            return update(s, k0, *carry)

        m0 = jnp.full((FOX_BQ, 1), MASK_VALUE, F32)
        acc0 = jnp.zeros((FOX_BQ, 2 * FOX_HEAD_DIM), F32)
        m, acc = lax.fori_loop(0, qi, kv_body, (m0, acc0))
        s, k0 = scores(qi)
        m, acc = update(jnp.where(causal, s, MASK_VALUE), k0, m, acc)
        o_ref[pl.ds(q0, FOX_BQ), :] = (acc[:, :FOX_HEAD_DIM] / acc[:, FOX_HEAD_DIM:]).astype(BF16)
        return 0

    lax.fori_loop(0, T_ROWS // FOX_BQ, q_body, 0)


def _fox_attention(proj, negc):
    cb = lambda col: col // FOX_HEAD_DIM
    blk = (T_ROWS, FOX_HEAD_DIM)
    nbytes = 2 * 4 * T_ROWS * FOX_HEAD_DIM * 2 + T_ROWS * 256 * 2 + 8 * FOX_BQ * FOX_BK * 4
    return pl.pallas_call(
        _fox_kernel,
        grid=(BATCH, FOX_HEADS),
        in_specs=[
            pl.BlockSpec(blk, lambda b, h: (b, cb(COL_FQ) + h)),
            pl.BlockSpec(blk, lambda b, h: (b, cb(COL_FK) + h)),
            pl.BlockSpec(blk, lambda b, h: (b, cb(COL_FV) + h)),
            pl.BlockSpec((1, 1, CUM_ROWS, LANES), lambda b, h: (b, h, 0, 0)),
        ],
        out_specs=pl.BlockSpec(blk, lambda b, h: (b, h)),
        out_shape=jax.ShapeDtypeStruct((M_ROWS, FOX_WIDTH), BF16),
        scratch_shapes=[pltpu.VMEM((T_ROWS, 2 * FOX_HEAD_DIM), BF16)],
        compiler_params=_params(("arbitrary", "arbitrary"), nbytes),
        name="fox_attn",
    )(proj, proj, proj, negc)


def _gla_kernel(q_ref, k_ref, v_ref, r_ref, small_ref, wa_ref, ba_ref, gn_ref, o_ref, st_ref, b_ref):
    C = GLA_CHUNK

    @pl.when(pl.program_id(1) == 0)
    def _():
        st_ref[...] = jnp.zeros_like(st_ref)

    z = _dot(small_ref[...].astype(BF16), wa_ref[...]) + ba_ref[...]
    g = _log_sigmoid(z) * (1.0 / GLA_TAU)
    ri = lax.broadcasted_iota(jnp.int32, (GLA_RB, GLA_RB), 0)
    rj = lax.broadcasted_iota(jnp.int32, (GLA_RB, GLA_RB), 1)
    same_chunk = (ri // C) == (rj // C)
    tri = jnp.where(same_chunk & (rj <= ri), 1.0, 0.0).astype(BF16)
    b_ref[...] = _dot_f32_rhs(tri, g, 2)

    ci = lax.broadcasted_iota(jnp.int32, (C, C), 0)
    cj = lax.broadcasted_iota(jnp.int32, (C, C), 1)
    tril = cj <= ci

    def chunk_body(c, _):
        r0 = pl.multiple_of(c * C, C)
        for hd in range(GLA_HEADS):
            ks = slice(hd * GLA_DK, (hd + 1) * GLA_DK)
            vs = slice(hd * GLA_DV, (hd + 1) * GLA_DV)
            b = b_ref[pl.ds(r0, C), ks]
            b_last = b[C - 1:C, :]
            qc = q_ref[pl.ds(r0, C), ks].astype(F32)
            kc = k_ref[pl.ds(r0, C), ks].astype(F32)
            vc = v_ref[pl.ds(r0, C), vs]
            q_dec = (qc * jnp.exp(b)).astype(BF16)
            k_inv = (kc * jnp.exp(-b)).astype(BF16)
            k_end = (kc * jnp.exp(b_last - b)).astype(BF16)
            a = jnp.where(tril, _dot_nt(q_dec, k_inv), 0.0).astype(BF16)
            st = st_ref[hd]
            o = _dot(a, vc) + _dot_nt(q_dec, st.astype(BF16))
            st_ref[hd] = st * jnp.exp(b_last) + _dot_tn(vc, k_end)
            rms = lax.rsqrt(jnp.mean(o * o, axis=-1, keepdims=True) + EPS)
            gr = r_ref[pl.ds(r0, C), vs].astype(F32)
            out = (o * rms) * gn_ref[:, vs] * (gr * _sigmoid(gr))
            o_ref[pl.ds(r0, C), vs] = out.astype(BF16)
        return 0

    lax.fori_loop(0, GLA_RB // C, chunk_body, 0)


def _gla(proj, small, wa_pad, b_alpha, gn_g):
    rb = GLA_RB
    nt = T_ROWS // rb
    row = lambda b, t: b * nt + t
    nbytes = (2 * rb * (2 * GLA_KEY_WIDTH + 2 * GLA_VAL_WIDTH) * 2 + 2 * rb * SMALL_W * 4
              + 2 * rb * GLA_VAL_WIDTH * 2 + GLA_HEADS * GLA_DV * GLA_DK * 4 + 4 * rb * GLA_KEY_WIDTH * 4
              + 2 * SMALL_W * GLA_KEY_WIDTH * 2)
    return pl.pallas_call(
        _gla_kernel,
        grid=(BATCH, nt),
        in_specs=[
            pl.BlockSpec((rb, GLA_KEY_WIDTH), lambda b, t: (row(b, t), COL_GQ // GLA_KEY_WIDTH)),
            pl.BlockSpec((rb, GLA_KEY_WIDTH), lambda b, t: (row(b, t), COL_GK // GLA_KEY_WIDTH)),
            pl.BlockSpec((rb, GLA_VAL_WIDTH), lambda b, t: (row(b, t), COL_GV // GLA_VAL_WIDTH)),
            pl.BlockSpec((rb, GLA_VAL_WIDTH), lambda b, t: (row(b, t), COL_GR // GLA_VAL_WIDTH)),
            pl.BlockSpec((rb, SMALL_W), lambda b, t: (row(b, t), 0)),
            pl.BlockSpec((SMALL_W, GLA_KEY_WIDTH), lambda b, t: (0, 0)),
            pl.BlockSpec((1, GLA_KEY_WIDTH), lambda b, t: (0, 0)),
            pl.BlockSpec((1, GLA_VAL_WIDTH), lambda b, t: (0, 0)),
        ],
        out_specs=pl.BlockSpec((rb, GLA_VAL_WIDTH), lambda b, t: (row(b, t), 0)),
        out_shape=jax.ShapeDtypeStruct((M_ROWS, GLA_VAL_WIDTH), BF16),
        scratch_shapes=[
            pltpu.VMEM((GLA_HEADS, GLA_DV, GLA_DK), F32),
            pltpu.VMEM((rb, GLA_KEY_WIDTH), F32),
        ],
        compiler_params=_params(("arbitrary", "arbitrary"), nbytes),
        name="gla",
    )(proj, proj, proj, proj, small, wa_pad, b_alpha, gn_g)


def _merge_kernel(of_ref, og_ref, wf_ref, wg_ref, gf_ref, gg_ref, y_ref):
    t_fox = _dot(of_ref[...], wf_ref[...])
    t_gla = _dot(og_ref[...], wg_ref[...])
    y = _sigmoid(gf_ref[...].astype(F32)) * t_fox + _sigmoid(gg_ref[...].astype(F32)) * t_gla
    y_ref[...] = y.astype(BF16)


def _merge(o_fox, o_gla, w_o_fox, w_o_gla, proj):
    bm, bn = MERGE_BM, MERGE_BN
    nbytes = (2 * bm * (FOX_WIDTH + GLA_VAL_WIDTH) * 2 + 2 * (FOX_WIDTH + GLA_VAL_WIDTH) * bn * 2
              + 6 * bm * bn * 2 + 4 * bm * bn * 4)
    return pl.pallas_call(
        _merge_kernel,
        grid=(M_ROWS // bm, D_MODEL // bn),
        in_specs=[
            pl.BlockSpec((bm, FOX_WIDTH), lambda i, j: (i, 0)),
            pl.BlockSpec((bm, GLA_VAL_WIDTH), lambda i, j: (i, 0)),
            pl.BlockSpec((FOX_WIDTH, bn), lambda i, j: (0, j)),
            pl.BlockSpec((GLA_VAL_WIDTH, bn), lambda i, j: (0, j)),
            pl.BlockSpec((bm, bn), lambda i, j: (i, COL_GATE_FOX // bn + j)),
            pl.BlockSpec((bm, bn), lambda i, j: (i, COL_GATE_GLA // bn + j)),
        ],
        out_specs=pl.BlockSpec((bm, bn), lambda i, j: (i, j)),
        out_shape=jax.ShapeDtypeStruct((M_ROWS, D_MODEL), BF16),
        compiler_params=_params(("arbitrary", "arbitrary"), nbytes),
        name="merge",
    )(o_fox, o_gla, w_o_fox, w_o_gla, proj, proj)


def _outproj_kernel(y_ref, w_ref, h_ref, o_ref):
    o_ref[...] = h_ref[...] + _dot(y_ref[...], w_ref[...])


def _outproj(y, w_out, h):
    bm, bn = OUTPROJ_BM, OUTPROJ_BN
    nbytes = 2 * bm * D_MODEL * 2 + 2 * D_MODEL * bn * 2 + 4 * bm * bn * 4 + bm * bn * 4
    return pl.pallas_call(
        _outproj_kernel,
        grid=(M_ROWS // bm, D_MODEL // bn),
        in_specs=[
            pl.BlockSpec((bm, D_MODEL), lambda i, j: (i, 0)),
            pl.BlockSpec((D_MODEL, bn), lambda i, j: (0, j)),
            pl.BlockSpec((bm, bn), lambda i, j: (i, j)),
        ],
        out_specs=pl.BlockSpec((bm, bn), lambda i, j: (i, j)),
        out_shape=jax.ShapeDtypeStruct((M_ROWS, D_MODEL), F32),
        compiler_params=_params(("arbitrary", "arbitrary"), nbytes),
        name="outproj",
    )(y, w_out, h)


def _mlp_kernel(h_ref, g_ref, w1_ref, w2_ref, fg_ref, o_ref, xn_ref, *, final_norm):
    j = pl.program_id(1)

    @pl.when(j == 0)
    def _():
        _rmsnorm_rows(h_ref, g_ref, xn_ref, MLP_BM)
        o_ref[...] = h_ref[...]

    u = jnp.maximum(_dot(xn_ref[...], w1_ref[...]), 0.0)
    o_ref[...] += _dot((u * u).astype(BF16), w2_ref[...])

    if final_norm:
        @pl.when(j == pl.num_programs(1) - 1)
        def _():
            _rmsnorm_rows(o_ref, fg_ref, o_ref, MLP_BM)


def _mlp(h, g, w1, w2, final_g, final_norm):
    bm, bf = MLP_BM, MLP_BF
    nbytes = (4 * bm * D_MODEL * 4 + bm * D_MODEL * 2 + 4 * D_MODEL * bf * 2
              + bm * bf * 6 + bm * D_MODEL * 4)
    return pl.pallas_call(
        functools.partial(_mlp_kernel, final_norm=final_norm),
        grid=(M_ROWS // bm, D_FF // bf),
        in_specs=[
            pl.BlockSpec((bm, D_MODEL), lambda i, j: (i, 0)),
            pl.BlockSpec((1, D_MODEL), lambda i, j: (0, 0)),
            pl.BlockSpec((D_MODEL, bf), lambda i, j: (0, j)),
            pl.BlockSpec((bf, D_MODEL), lambda i, j: (j, 0)),
            pl.BlockSpec((1, D_MODEL), lambda i, j: (0, 0)),
        ],
        out_specs=pl.BlockSpec((bm, D_MODEL), lambda i, j: (i, 0)),
        out_shape=jax.ShapeDtypeStruct((M_ROWS, D_MODEL), F32),
        scratch_shapes=[pltpu.VMEM((bm, D_MODEL), BF16)],
        compiler_params=_params(("arbitrary", "arbitrary"), nbytes),
        name="mlp_final" if final_norm else "mlp",
    )(h, g, w1, w2, final_g)


def _prep_in_weights(w_in):
    o = 0
    pieces = {}
    for name, width in (("fq", FOX_WIDTH), ("fk", FOX_WIDTH), ("fv", FOX_WIDTH), ("ff", FOX_HEADS),
                        ("gq", GLA_KEY_WIDTH), ("gk", GLA_KEY_WIDTH), ("gv", GLA_VAL_WIDTH),
                        ("gr", GLA_VAL_WIDTH), ("ga", GLA_RANK), ("gate_fox", D_MODEL), ("gate_gla", D_MODEL)):
        pieces[name] = w_in[:, o:o + width]
        o += width
    main = jnp.concatenate([
        pieces["gv"], pieces["gr"], pieces["gate_fox"], pieces["gate_gla"],
        pieces["fq"] * (FOX_HEAD_DIM ** -0.5), pieces["fk"], pieces["fv"],
        pieces["gq"] * (GLA_DK ** -0.5), pieces["gk"]], axis=1).astype(BF16)
    small = jnp.concatenate([
        pieces["ga"], pieces["ff"],
        jnp.zeros((D_MODEL, SMALL_W - GLA_RANK - FOX_HEADS), F32)], axis=1).astype(BF16)
    return main, small


def kernel(x, meta_tokens, norm_mix_g, w_in, b_forget, w_alpha2, b_alpha, gla_norm_g, w_o_fox, w_o_gla,
           w_out, norm_mlp_g, w_ff1, w_ff2, final_norm_g):
    meta = jnp.broadcast_to(meta_tokens[None].astype(F32), (BATCH, N_META, D_MODEL))
    pad = jnp.zeros((BATCH, T_ROWS - N_META - SEQ, D_MODEL), F32)
    h = jnp.concatenate([meta, x.astype(F32), pad], axis=1).reshape(M_ROWS, D_MODEL)
    final_g = final_norm_g.reshape(1, D_MODEL)

    for l in range(DEPTH):
        w_main, w_small = _prep_in_weights(w_in[l])
        wa_pad = jnp.concatenate(
            [w_alpha2[l], jnp.zeros((SMALL_W - GLA_RANK, GLA_KEY_WIDTH), F32)], axis=0).astype(BF16)

        proj, small, fft = _inproj(h, norm_mix_g[l].reshape(1, D_MODEL), w_main, w_small)
        ff_rows = fft.reshape(FOX_HEADS, BATCH, T_LANE_ROWS, LANES)
        ff_rows = jnp.pad(ff_rows, ((0, 0), (0, 0), (0, CUM_ROWS - T_LANE_ROWS), (0, 0)))
        negc = _fox_cumsum(b_forget[l], ff_rows)
        o_fox = _fox_attention(proj, negc)
        o_gla = _gla(proj, small, wa_pad, b_alpha[l].reshape(1, GLA_KEY_WIDTH),
                     gla_norm_g[l].reshape(1, GLA_VAL_WIDTH))
        y = _merge(o_fox, o_gla, w_o_fox[l].astype(BF16), w_o_gla[l].astype(BF16), proj)
        h = _outproj(y, w_out[l].astype(BF16), h)
        h = _mlp(h, norm_mlp_g[l].reshape(1, D_MODEL), w_ff1[l].astype(BF16), w_ff2[l].astype(BF16),
                 final_g, final_norm=(l == DEPTH - 1))

    return h.reshape(BATCH, T_ROWS, D_MODEL)[:, N_META:N_META + SEQ]
```

```python
import functools

import jax
import jax.numpy as jnp
from jax import lax
from jax.experimental import pallas as pl
from jax.experimental.pallas import tpu as pltpu

F32 = jnp.float32
BF16 = jnp.bfloat16

D_MODEL = 2048
BATCH = 4
SEQ = 4096
DEPTH = 2
N_META = 16
EPS = 1e-6
MASK_VALUE = -1e30
LOG2E = 1.4426950408889634

FOX_HEADS = 8
FOX_HEAD_DIM = 128
FOX_WIDTH = FOX_HEADS * FOX_HEAD_DIM
GLA_HEADS = 4
GLA_DK = 256
GLA_DV = 512
GLA_KEY_WIDTH = GLA_HEADS * GLA_DK
GLA_VAL_WIDTH = GLA_HEADS * GLA_DV
GLA_RANK = 16
GLA_TAU = 16.0
GLA_CHUNK = 64
D_FF = 4 * D_MODEL

LANES = 128
V7X_VMEM_LIMIT_CAP = 60 * 1024 * 1024

T_ROWS = N_META + SEQ + (LANES - N_META)
M_ROWS = BATCH * T_ROWS
T_LANE_ROWS = T_ROWS // LANES
CUM_ROWS = 64

COL_FQ = 0
COL_FK = COL_FQ + FOX_WIDTH
COL_FV = COL_FK + FOX_WIDTH
COL_GQ = COL_FV + FOX_WIDTH
COL_GK = COL_GQ + GLA_KEY_WIDTH
COL_GV = COL_GK + GLA_KEY_WIDTH
COL_GR = COL_GV + GLA_VAL_WIDTH
COL_GATE_FOX = COL_GR + GLA_VAL_WIDTH
COL_GATE_GLA = COL_GATE_FOX + D_MODEL
N_PROJ = COL_GATE_GLA + D_MODEL
WIN_FF0 = 3 * FOX_WIDTH
WIN_GQ0 = WIN_FF0 + FOX_HEADS
WIN_GA0 = WIN_GQ0 + 2 * GLA_KEY_WIDTH + 2 * GLA_VAL_WIDTH
WIN_GATE0 = WIN_GA0 + GLA_RANK
SMALL_W = LANES
SMALL_FF0 = GLA_RANK

NORM_CHUNK = 128
INPROJ_BM, INPROJ_BN = 1408, 1024
MERGE_BM, MERGE_BN = 1408, 512
OUTPROJ_BM, OUTPROJ_BN = 1408, 512
MLP_BM, MLP_BF = 768, 512
FOX_NH = 4
FOX_BQ = 384
FOX_BK = 384
FOX_STRIP = 64
GLA_RB = 384


def _vmem_limit(nbytes):
    return int(min(V7X_VMEM_LIMIT_CAP, nbytes + (4 << 20)))


def _params(sem, nbytes):
    return pltpu.CompilerParams(dimension_semantics=sem, vmem_limit_bytes=_vmem_limit(nbytes))


def _layer_spec(l, shape, index_map):
    return pl.BlockSpec((None,) + shape, lambda *idx: (l,) + index_map(*idx))


def _rmsnorm_rows(src_ref, g_ref, dst_ref, rows):
    g = g_ref[...]

    def body(c, _):
        r0 = pl.multiple_of(c * NORM_CHUNK, NORM_CHUNK)
        x = src_ref[pl.ds(r0, NORM_CHUNK), :]
        ms = jnp.mean(x * x, axis=-1, keepdims=True)
        dst_ref[pl.ds(r0, NORM_CHUNK), :] = ((x * lax.rsqrt(ms + EPS)) * g).astype(dst_ref.dtype)
        return 0

    lax.fori_loop(0, rows // NORM_CHUNK, body, 0)


def _log_sigmoid(x):
    return jnp.minimum(x, 0.0) - jnp.log(1.0 + jnp.exp(-jnp.abs(x)))


def _sigmoid(x):
    return 1.0 / (1.0 + jnp.exp(-x))


def _dot(a, b):
    return jnp.dot(a, b, preferred_element_type=F32)


def _dot_nt(a, b):
    return lax.dot_general(a, b, (((1,), (1,)), ((), ())), preferred_element_type=F32)


def _dot_tn(a, b):
    return lax.dot_general(a, b, (((0,), (0,)), ((), ())), preferred_element_type=F32)


def _split_bf16(x, n):
    pieces = []
    rem = x
    for _ in range(n):
        hi = rem.astype(BF16)
        pieces.append(hi)
        rem = rem - hi.astype(F32)
    return pieces


def _dot_f32_lhs(x, r, n):
    acc = None
    for p in _split_bf16(x, n):
        t = _dot(p, r)
        acc = t if acc is None else acc + t
    return acc


def _dot_f32_rhs(l, x, n):
    acc = None
    for p in _split_bf16(x, n):
        t = _dot(l, p)
        acc = t if acc is None else acc + t
    return acc


def _inproj_kernel(h_ref, g_ref, w_ref, ws_ref, p_ref, small_ref, fft_ref, xn_ref):
    @pl.when(pl.program_id(1) == 0)
    def _():
        _rmsnorm_rows(h_ref, g_ref, xn_ref, INPROJ_BM)
        sm = _dot(xn_ref[...], ws_ref[...])
        small_ref[...] = sm
        fft_ref[...] = sm.T[SMALL_FF0:SMALL_FF0 + FOX_HEADS, :]

    p_ref[...] = _dot(xn_ref[...], w_ref[...]).astype(BF16)


def _inproj(l, h, g, w_main, w_small):
    bm, bn = INPROJ_BM, INPROJ_BN
    nbytes = (2 * bm * D_MODEL * 4 + 2 * D_MODEL * bn * 2 + 2 * D_MODEL * SMALL_W * 2
              + 2 * bm * bn * 2 + 2 * bm * SMALL_W * 4 + bm * D_MODEL * 2 + bm * bn * 4 + 2 * 8 * bm * 4)
    return pl.pallas_call(
        _inproj_kernel,
        grid=(M_ROWS // bm, N_PROJ // bn),
        in_specs=[
            pl.BlockSpec((bm, D_MODEL), lambda i, j: (i, 0)),
            _layer_spec(l, (1, D_MODEL), lambda i, j: (0, 0)),
            _layer_spec(l, (D_MODEL, bn), lambda i, j: (0, j)),
            _layer_spec(l, (D_MODEL, SMALL_W), lambda i, j: (0, 0)),
        ],
        out_specs=[
            pl.BlockSpec((bm, bn), lambda i, j: (i, j)),
            pl.BlockSpec((bm, SMALL_W), lambda i, j: (i, 0)),
            pl.BlockSpec((FOX_HEADS, bm), lambda i, j: (0, i)),
        ],
        out_shape=[
            jax.ShapeDtypeStruct((M_ROWS, N_PROJ), BF16),
            jax.ShapeDtypeStruct((M_ROWS, SMALL_W), F32),
            jax.ShapeDtypeStruct((FOX_HEADS, M_ROWS), F32),
        ],
        scratch_shapes=[pltpu.VMEM((bm, D_MODEL), BF16)],
        compiler_params=_params(("arbitrary", "arbitrary"), nbytes),
        name="inproj",
    )(h, g, w_main, w_small)


def _fox_cumsum_kernel(bf_ref, ff_ref, nc_ref, *, layer):
    li = lax.broadcasted_iota(jnp.int32, (LANES, LANES), 0)
    lj = lax.broadcasted_iota(jnp.int32, (LANES, LANES), 1)
    upper = jnp.where(li <= lj, 1.0, 0.0).astype(BF16)
    ones = jnp.ones((LANES, LANES), BF16)
    ri = lax.broadcasted_iota(jnp.int32, (CUM_ROWS, CUM_ROWS), 0)
    rj = lax.broadcasted_iota(jnp.int32, (CUM_ROWS, CUM_ROWS), 1)
    strict_lower = jnp.where(rj < ri, 1.0, 0.0).astype(BF16)
    for hd in range(FOX_HEADS):
        lf = _log_sigmoid(ff_ref[hd, 0] + bf_ref[layer, hd])
        c = _dot_f32_lhs(lf, upper, 3) + _dot_f32_rhs(strict_lower, _dot_f32_lhs(lf, ones, 3), 3)
        nc_ref[0, hd] = c * (-LOG2E)


def _fox_cumsum(l, b_forget, ff_rows):
    return pl.pallas_call(
        functools.partial(_fox_cumsum_kernel, layer=l),
        grid=(BATCH,),
        in_specs=[
            pl.BlockSpec(memory_space=pltpu.SMEM),
            pl.BlockSpec((FOX_HEADS, 1, CUM_ROWS, LANES), lambda b: (0, b, 0, 0)),
        ],
        out_specs=pl.BlockSpec((1, FOX_HEADS, CUM_ROWS, LANES), lambda b: (b, 0, 0, 0)),
        out_shape=jax.ShapeDtypeStruct((BATCH, FOX_HEADS, CUM_ROWS, LANES), F32),
        compiler_params=_params(("arbitrary",), 8 << 20),
        name="fox_cumsum",
    )(b_forget, ff_rows)


def _fox_kernel(q_ref, k_ref, v_ref, nc_ref, o_ref, vx_ref, s_ref, p_ref, m_ref, al_ref, acc_ref):
    hd_cols = [slice(hh * FOX_HEAD_DIM, (hh + 1) * FOX_HEAD_DIM) for hh in range(FOX_NH)]
    for hh in range(FOX_NH):
        vx_ref[hh, :, :FOX_HEAD_DIM] = v_ref[:, hd_cols[hh]]
        vx_ref[hh, :, FOX_HEAD_DIM:] = jnp.ones((T_ROWS, FOX_HEAD_DIM), BF16)
    lane_chunks = FOX_BK // LANES
    row_id = lax.broadcasted_iota(jnp.int32, (FOX_BQ, FOX_BK), 0)
    col_id = lax.broadcasted_iota(jnp.int32, (FOX_BQ, FOX_BK), 1)
    causal = col_id <= row_id

    def q_body(qi, _):
        q0 = pl.multiple_of(qi * FOX_BQ, FOX_BQ)
        m_ref[...] = jnp.full(m_ref.shape, MASK_VALUE, F32)
        acc_ref[...] = jnp.zeros(acc_ref.shape, F32)

        def tile(kj, diagonal):
            k0 = pl.multiple_of(kj * FOX_BK, FOX_BK)
            for hh in range(FOX_NH):
                s = _dot_nt(q_ref[pl.ds(q0, FOX_BQ), hd_cols[hh]], k_ref[pl.ds(k0, FOX_BK), hd_cols[hh]])
                s = s + jnp.concatenate(
                    [nc_ref[0, hh, pl.ds(kj * lane_chunks + c, 1), :] for c in range(lane_chunks)], axis=1)
                s_ref[hh] = jnp.where(causal, s, MASK_VALUE) if diagonal else s
            for hh in range(FOX_NH):
                for r in range(0, FOX_BQ, FOX_STRIP):
                    rows = slice(r, r + FOX_STRIP)
                    sv = s_ref[hh, rows, :]
                    m_old = m_ref[hh, rows, :]
                    m_new = jnp.maximum(m_old, jnp.max(sv, axis=1, keepdims=True))
                    m_ref[hh, rows, :] = m_new
                    al_ref[hh, rows, :] = jnp.exp2(m_old - m_new)
                    p_ref[hh, rows, :] = jnp.exp2(sv - jnp.concatenate([m_new] * lane_chunks, axis=1)).astype(BF16)
            for hh in range(FOX_NH):
                alpha = jnp.concatenate([al_ref[hh]] * 2, axis=1)
                acc_ref[hh] = alpha * acc_ref[hh] + _dot(p_ref[hh], vx_ref[hh, pl.ds(k0, FOX_BK), :])

        def kv_body(kj, _):
            tile(kj, False)
            return 0

        lax.fori_loop(0, qi, kv_body, 0)
        tile(qi, True)
        for hh in range(FOX_NH):
            acc = acc_ref[hh]
            o_ref[pl.ds(q0, FOX_BQ), hd_cols[hh]] = (
                acc[:, :FOX_HEAD_DIM] / acc[:, FOX_HEAD_DIM:]).astype(BF16)
        return 0

    lax.fori_loop(0, T_ROWS // FOX_BQ, q_body, 0)


def _fox_attention(proj, negc):
    w = FOX_NH * FOX_HEAD_DIM
    blk = (T_ROWS, w)
    nbytes = (2 * 4 * T_ROWS * w * 2 + FOX_NH * T_ROWS * 2 * FOX_HEAD_DIM * 2
              + FOX_NH * 6 * FOX_BQ * FOX_BK * 4 + 2 * FOX_NH * CUM_ROWS * LANES * 4)
    return pl.pallas_call(
        _fox_kernel,
        grid=(BATCH, FOX_HEADS // FOX_NH),
        in_specs=[
            pl.BlockSpec(blk, lambda b, g: (b, COL_FQ // w + g)),
            pl.BlockSpec(blk, lambda b, g: (b, COL_FK // w + g)),
            pl.BlockSpec(blk, lambda b, g: (b, COL_FV // w + g)),
            pl.BlockSpec((1, FOX_NH, CUM_ROWS, LANES), lambda b, g: (b, g, 0, 0)),
        ],
        out_specs=pl.BlockSpec(blk, lambda b, g: (b, g)),
        out_shape=jax.ShapeDtypeStruct((M_ROWS, FOX_WIDTH), BF16),
        scratch_shapes=[
            pltpu.VMEM((FOX_NH, T_ROWS, 2 * FOX_HEAD_DIM), BF16),
            pltpu.VMEM((FOX_NH, FOX_BQ, FOX_BK), F32),
            pltpu.VMEM((FOX_NH, FOX_BQ, FOX_BK), BF16),
            pltpu.VMEM((FOX_NH, FOX_BQ, LANES), F32),
            pltpu.VMEM((FOX_NH, FOX_BQ, LANES), F32),
            pltpu.VMEM((FOX_NH, FOX_BQ, 2 * FOX_HEAD_DIM), F32),
        ],
        compiler_params=_params(("arbitrary", "arbitrary"), nbytes),
        name="fox_attn",
    )(proj, proj, proj, negc)


def _gla_kernel(*refs):
    q_ref, k_ref = refs[0], refs[1]
    v_refs = refs[2:2 + GLA_HEADS]
    r_refs = refs[2 + GLA_HEADS:2 + 2 * GLA_HEADS]
    small_ref, wa_ref, ba_ref, gn_ref, o_ref, st_ref, b_ref = refs[2 + 2 * GLA_HEADS:]
    C = GLA_CHUNK
    log2_c = C.bit_length() - 1

    @pl.when(pl.program_id(1) == 0)
    def _():
        st_ref[...] = jnp.zeros_like(st_ref)

    z = _dot(small_ref[...].astype(BF16), wa_ref[...]) + ba_ref[...]
    g = _log_sigmoid(z) * (1.0 / GLA_TAU)
    ri = lax.broadcasted_iota(jnp.int32, (GLA_RB, GLA_RB), 0)
    rj = lax.broadcasted_iota(jnp.int32, (GLA_RB, GLA_RB), 1)
    same_chunk = jnp.right_shift(ri, log2_c) == jnp.right_shift(rj, log2_c)
    tri = jnp.where(same_chunk & (rj <= ri), 1.0, 0.0).astype(BF16)
    b_ref[...] = _dot_f32_rhs(tri, g, 2)

    ci = lax.broadcasted_iota(jnp.int32, (C, C), 0)
    cj = lax.broadcasted_iota(jnp.int32, (C, C), 1)
    tril = cj <= ci

    def chunk_body(c, _):
        r0 = pl.multiple_of(c * C, C)
        for hd in range(GLA_HEADS):
            ks = slice(hd * GLA_DK, (hd + 1) * GLA_DK)
            vs = slice(hd * GLA_DV, (hd + 1) * GLA_DV)
            b = b_ref[pl.ds(r0, C), ks]
            b_last = b[C - 1:C, :]
            qc = q_ref[pl.ds(r0, C), ks].astype(F32)
            kc = k_ref[pl.ds(r0, C), ks].astype(F32)
            vc = v_refs[hd][pl.ds(r0, C), :]
            q_dec = (qc * jnp.exp(b)).astype(BF16)
            k_inv = (kc * jnp.exp(-b)).astype(BF16)
            k_end = (kc * jnp.exp(b_last - b)).astype(BF16)
            a = jnp.where(tril, _dot_nt(q_dec, k_inv), 0.0).astype(BF16)
            st = st_ref[hd]
            o = _dot(a, vc) + _dot_nt(q_dec, st.astype(BF16))
            st_ref[hd] = st * jnp.exp(b_last) + _dot_tn(vc, k_end)
            rms = lax.rsqrt(jnp.mean(o * o, axis=-1, keepdims=True) + EPS)
            gr = r_refs[hd][pl.ds(r0, C), :].astype(F32)
            out = (o * rms) * gn_ref[:, vs] * (gr * _sigmoid(gr))
            o_ref[pl.ds(r0, C), vs] = out.astype(BF16)
        return 0

    lax.fori_loop(0, GLA_RB // C, chunk_body, 0)


def _gla(l, proj, small, wa_pad, b_alpha, gn_g):
    rb = GLA_RB
    nt = T_ROWS // rb
    row = lambda b, t: b * nt + t
    nbytes = (2 * rb * (2 * GLA_KEY_WIDTH + 2 * GLA_VAL_WIDTH) * 2 + 2 * rb * SMALL_W * 4
              + 2 * rb * GLA_VAL_WIDTH * 2 + GLA_HEADS * GLA_DV * GLA_DK * 4 + 4 * rb * GLA_KEY_WIDTH * 4
              + 2 * SMALL_W * GLA_KEY_WIDTH * 2)

    def head_specs(col0):
        return [pl.BlockSpec((rb, GLA_DV), functools.partial(lambda b, t, c: (row(b, t), c), c=col0 // GLA_DV + hd))
                for hd in range(GLA_HEADS)]

    return pl.pallas_call(
        _gla_kernel,
        grid=(BATCH, nt),
        in_specs=[
            pl.BlockSpec((rb, GLA_KEY_WIDTH), lambda b, t: (row(b, t), COL_GQ // GLA_KEY_WIDTH)),
            pl.BlockSpec((rb, GLA_KEY_WIDTH), lambda b, t: (row(b, t), COL_GK // GLA_KEY_WIDTH)),
            *head_specs(COL_GV),
            *head_specs(COL_GR),
            pl.BlockSpec((rb, SMALL_W), lambda b, t: (row(b, t), 0)),
            _layer_spec(l, (SMALL_W, GLA_KEY_WIDTH), lambda b, t: (0, 0)),
            _layer_spec(l, (1, GLA_KEY_WIDTH), lambda b, t: (0, 0)),
            _layer_spec(l, (1, GLA_VAL_WIDTH), lambda b, t: (0, 0)),
        ],
        out_specs=pl.BlockSpec((rb, GLA_VAL_WIDTH), lambda b, t: (row(b, t), 0)),
        out_shape=jax.ShapeDtypeStruct((M_ROWS, GLA_VAL_WIDTH), BF16),
        scratch_shapes=[
            pltpu.VMEM((GLA_HEADS, GLA_DV, GLA_DK), F32),
            pltpu.VMEM((rb, GLA_KEY_WIDTH), F32),
        ],
        compiler_params=_params(("arbitrary", "arbitrary"), nbytes),
        name="gla",
    )(*([proj] * (2 + 2 * GLA_HEADS)), small, wa_pad, b_alpha, gn_g)


def _merge_kernel(of_ref, og_ref, wf_ref, wg_ref, gf_ref, gg_ref, y_ref):
    t_fox = _dot(of_ref[...], wf_ref[...])
    t_gla = _dot(og_ref[...], wg_ref[...])
    y = _sigmoid(gf_ref[...].astype(F32)) * t_fox + _sigmoid(gg_ref[...].astype(F32)) * t_gla
    y_ref[...] = y.astype(BF16)


def _merge(l, o_fox, o_gla, w_o_fox, w_o_gla, proj):
    bm, bn = MERGE_BM, MERGE_BN
    nbytes = (2 * bm * (FOX_WIDTH + GLA_VAL_WIDTH) * 2 + 2 * (FOX_WIDTH + GLA_VAL_WIDTH) * bn * 2
              + 6 * bm * bn * 2 + 4 * bm * bn * 4)
    return pl.pallas_call(
        _merge_kernel,
        grid=(M_ROWS // bm, D_MODEL // bn),
        in_specs=[
            pl.BlockSpec((bm, FOX_WIDTH), lambda i, j: (i, 0)),
            pl.BlockSpec((bm, GLA_VAL_WIDTH), lambda i, j: (i, 0)),
            _layer_spec(l, (FOX_WIDTH, bn), lambda i, j: (0, j)),
            _layer_spec(l, (GLA_VAL_WIDTH, bn), lambda i, j: (0, j)),
            pl.BlockSpec((bm, bn), lambda i, j: (i, COL_GATE_FOX // bn + j)),
            pl.BlockSpec((bm, bn), lambda i, j: (i, COL_GATE_GLA // bn + j)),
        ],
        out_specs=pl.BlockSpec((bm, bn), lambda i, j: (i, j)),
        out_shape=jax.ShapeDtypeStruct((M_ROWS, D_MODEL), BF16),
        compiler_params=_params(("arbitrary", "arbitrary"), nbytes),
        name="merge",
    )(o_fox, o_gla, w_o_fox, w_o_gla, proj, proj)


def _outproj_kernel(y_ref, w_ref, h_ref, o_ref):
    o_ref[...] = h_ref[...] + _dot(y_ref[...], w_ref[...])


def _outproj(l, y, w_out, h):
    bm, bn = OUTPROJ_BM, OUTPROJ_BN
    nbytes = 2 * bm * D_MODEL * 2 + 2 * D_MODEL * bn * 2 + 4 * bm * bn * 4 + bm * bn * 4
    return pl.pallas_call(
        _outproj_kernel,
        grid=(M_ROWS // bm, D_MODEL // bn),
        in_specs=[
            pl.BlockSpec((bm, D_MODEL), lambda i, j: (i, 0)),
            _layer_spec(l, (D_MODEL, bn), lambda i, j: (0, j)),
            pl.BlockSpec((bm, bn), lambda i, j: (i, j)),
        ],
        out_specs=pl.BlockSpec((bm, bn), lambda i, j: (i, j)),
        out_shape=jax.ShapeDtypeStruct((M_ROWS, D_MODEL), F32),
        compiler_params=_params(("arbitrary", "arbitrary"), nbytes),
        name="outproj",
    )(y, w_out, h)


def _mlp_kernel(h_ref, g_ref, w1_ref, w2_ref, fg_ref, o_ref, xn_ref, *, final_norm):
    j = pl.program_id(1)

    @pl.when(j == 0)
    def _():
        _rmsnorm_rows(h_ref, g_ref, xn_ref, MLP_BM)
        o_ref[...] = h_ref[...]

    u = jnp.maximum(_dot(xn_ref[...], w1_ref[...]), 0.0)
    o_ref[...] += _dot((u * u).astype(BF16), w2_ref[...])

    if final_norm:
        @pl.when(j == pl.num_programs(1) - 1)
        def _():
            _rmsnorm_rows(o_ref, fg_ref, o_ref, MLP_BM)


def _mlp(l, h, g, w1, w2, final_g, final_norm):
    bm, bf = MLP_BM, MLP_BF
    nbytes = (4 * bm * D_MODEL * 4 + bm * D_MODEL * 2 + 4 * D_MODEL * bf * 2
              + bm * bf * 6 + bm * D_MODEL * 4)
    return pl.pallas_call(
        functools.partial(_mlp_kernel, final_norm=final_norm),
        grid=(M_ROWS // bm, D_FF // bf),
        in_specs=[
            pl.BlockSpec((bm, D_MODEL), lambda i, j: (i, 0)),
            _layer_spec(l, (1, D_MODEL), lambda i, j: (0, 0)),
            _layer_spec(l, (D_MODEL, bf), lambda i, j: (0, j)),
            _layer_spec(l, (bf, D_MODEL), lambda i, j: (j, 0)),
            pl.BlockSpec((1, D_MODEL), lambda i, j: (0, 0)),
        ],
        out_specs=pl.BlockSpec((bm, D_MODEL), lambda i, j: (i, 0)),
        out_shape=jax.ShapeDtypeStruct((M_ROWS, D_MODEL), F32),
        scratch_shapes=[pltpu.VMEM((bm, D_MODEL), BF16)],
        compiler_params=_params(("arbitrary", "arbitrary"), nbytes),
        name="mlp_final" if final_norm else "mlp",
    )(h, g, w1, w2, final_g)


def _prep_in_weights(w_in):
    fox = w_in[:, :, :WIN_FF0]
    fox_scale = jnp.concatenate([jnp.full((FOX_WIDTH,), FOX_HEAD_DIM ** -0.5 * LOG2E, F32),
                                 jnp.ones((2 * FOX_WIDTH,), F32)])
    gla = w_in[:, :, WIN_GQ0:WIN_GA0]
    gla_scale = jnp.concatenate([jnp.full((GLA_KEY_WIDTH,), GLA_DK ** -0.5, F32),
                                 jnp.ones((GLA_KEY_WIDTH + 2 * GLA_VAL_WIDTH,), F32)])
    gates = w_in[:, :, WIN_GATE0:]
    main = jnp.concatenate([(fox * fox_scale).astype(BF16), (gla * gla_scale).astype(BF16),
                            gates.astype(BF16)], axis=2)
    small = jnp.concatenate([
        w_in[:, :, WIN_GA0:WIN_GATE0], w_in[:, :, WIN_FF0:WIN_GQ0],
        jnp.zeros((DEPTH, D_MODEL, SMALL_W - GLA_RANK - FOX_HEADS), F32)], axis=2).astype(BF16)
    return main, small


def kernel(x, meta_tokens, norm_mix_g, w_in, b_forget, w_alpha2, b_alpha, gla_norm_g, w_o_fox, w_o_gla,
           w_out, norm_mlp_g, w_ff1, w_ff2, final_norm_g):
    meta = jnp.broadcast_to(meta_tokens[None].astype(F32), (BATCH, N_META, D_MODEL))
    pad = jnp.zeros((BATCH, T_ROWS - N_META - SEQ, D_MODEL), F32)
    h = jnp.concatenate([meta, x.astype(F32), pad], axis=1).reshape(M_ROWS, D_MODEL)

    w_main, w_small = _prep_in_weights(w_in)
    wa_pad = jnp.pad(w_alpha2, ((0, 0), (0, SMALL_W - GLA_RANK), (0, 0))).astype(BF16)
    w_o_fox_b, w_o_gla_b, w_out_b = w_o_fox.astype(BF16), w_o_gla.astype(BF16), w_out.astype(BF16)
    w_ff1_b, w_ff2_b = w_ff1.astype(BF16), w_ff2.astype(BF16)
    norm_mix_g3 = norm_mix_g.reshape(DEPTH, 1, D_MODEL)
    norm_mlp_g3 = norm_mlp_g.reshape(DEPTH, 1, D_MODEL)
    b_alpha3 = b_alpha.reshape(DEPTH, 1, GLA_KEY_WIDTH)
    gla_norm_g3 = gla_norm_g.reshape(DEPTH, 1, GLA_VAL_WIDTH)
    final_g = final_norm_g.reshape(1, D_MODEL)

    for l in range(DEPTH):
        proj, small, fft = _inproj(l, h, norm_mix_g3, w_main, w_small)
        ff_rows = fft.reshape(FOX_HEADS, BATCH, T_LANE_ROWS, LANES)
        ff_rows = jnp.pad(ff_rows, ((0, 0), (0, 0), (0, CUM_ROWS - T_LANE_ROWS), (0, 0)))
        negc = _fox_cumsum(l, b_forget, ff_rows)
        o_fox = _fox_attention(proj, negc)
        o_gla = _gla(l, proj, small, wa_pad, b_alpha3, gla_norm_g3)
        y = _merge(l, o_fox, o_gla, w_o_fox_b, w_o_gla_b, proj)
        h = _outproj(l, y, w_out_b, h)
        h = _mlp(l, h, norm_mlp_g3, w_ff1_b, w_ff2_b, final_g, final_norm=(l == DEPTH - 1))

    return h.reshape(BATCH, T_ROWS, D_MODEL)[:, N_META:N_META + SEQ]
```

```python
import functools

import jax
import jax.numpy as jnp
from jax import lax
from jax.experimental import pallas as pl
from jax.experimental.pallas import tpu as pltpu

F32 = jnp.float32
BF16 = jnp.bfloat16

D_MODEL = 2048
BATCH = 4
SEQ = 4096
DEPTH = 2
N_META = 16
EPS = 1e-6
MASK_VALUE = -1e30
LOG2E = 1.4426950408889634

FOX_HEADS = 8
FOX_HEAD_DIM = 128
FOX_WIDTH = FOX_HEADS * FOX_HEAD_DIM
GLA_HEADS = 4
GLA_DK = 256
GLA_DV = 512
GLA_KEY_WIDTH = GLA_HEADS * GLA_DK
GLA_VAL_WIDTH = GLA_HEADS * GLA_DV
GLA_RANK = 16
GLA_TAU = 16.0
GLA_CHUNK = 64
D_FF = 4 * D_MODEL

LANES = 128
V7X_VMEM_LIMIT_CAP = 60 * 1024 * 1024

T_ROWS = N_META + SEQ + (LANES - N_META)
M_ROWS = BATCH * T_ROWS
T_LANE_ROWS = T_ROWS // LANES
CUM_ROWS = 64

COL_FQ = 0
COL_FK = COL_FQ + FOX_WIDTH
COL_FV = COL_FK + FOX_WIDTH
COL_GQ = COL_FV + FOX_WIDTH
COL_GK = COL_GQ + GLA_KEY_WIDTH
COL_GV = COL_GK + GLA_KEY_WIDTH
COL_GR = COL_GV + GLA_VAL_WIDTH
COL_GATE_FOX = COL_GR + GLA_VAL_WIDTH
COL_GATE_GLA = COL_GATE_FOX + D_MODEL
N_PROJ = COL_GATE_GLA + D_MODEL
WIN_FF0 = 3 * FOX_WIDTH
WIN_GQ0 = WIN_FF0 + FOX_HEADS
WIN_GA0 = WIN_GQ0 + 2 * GLA_KEY_WIDTH + 2 * GLA_VAL_WIDTH
WIN_GATE0 = WIN_GA0 + GLA_RANK
SMALL_W = LANES
SMALL_FF0 = GLA_RANK

NORM_CHUNK = 128
INPROJ_BM, INPROJ_BN = 1408, 1024
MERGE_BM, MERGE_BN = 1408, 512
OUTPROJ_BM, OUTPROJ_BN = 1408, 1024
MLP_BM, MLP_BF = 1056, 1024
MLP_NORM_CHUNK = 176
FOX_NH = 4
FOX_BQ = 384
FOX_BK = 384
FOX_STRIP = 64
GLA_RB = 384
WPREP_ROWS = 128


def _vmem_limit(nbytes):
    return int(min(V7X_VMEM_LIMIT_CAP, nbytes + (4 << 20)))


def _params(sem, nbytes):
    return pltpu.CompilerParams(dimension_semantics=sem, vmem_limit_bytes=_vmem_limit(nbytes))


def _layer_spec(l, shape, index_map):
    return pl.BlockSpec((None,) + shape, lambda *idx: (l,) + index_map(*idx))


def _rmsnorm_rows(src_ref, g_ref, dst_ref, rows, chunk=NORM_CHUNK):
    g = g_ref[...]

    def body(c, _):
        r0 = pl.multiple_of(c * chunk, chunk)
        x = src_ref[pl.ds(r0, chunk), :]
        ms = jnp.mean(x * x, axis=-1, keepdims=True)
        dst_ref[pl.ds(r0, chunk), :] = ((x * lax.rsqrt(ms + EPS)) * g).astype(dst_ref.dtype)
        return 0

    lax.fori_loop(0, rows // chunk, body, 0)


def _log_sigmoid(x):
    return jnp.minimum(x, 0.0) - jnp.log(1.0 + jnp.exp(-jnp.abs(x)))


def _sigmoid(x):
    return 1.0 / (1.0 + jnp.exp(-x))


def _dot(a, b):
    return jnp.dot(a, b, preferred_element_type=F32)


def _dot_nt(a, b):
    return lax.dot_general(a, b, (((1,), (1,)), ((), ())), preferred_element_type=F32)


def _dot_tn(a, b):
    return lax.dot_general(a, b, (((0,), (0,)), ((), ())), preferred_element_type=F32)


def _split_bf16(x, n):
    pieces = []
    rem = x
    for _ in range(n):
        hi = rem.astype(BF16)
        pieces.append(hi)
        rem = rem - hi.astype(F32)
    return pieces


def _dot_f32_lhs(x, r, n):
    acc = None
    for p in _split_bf16(x, n):
        t = _dot(p, r)
        acc = t if acc is None else acc + t
    return acc


def _dot_f32_rhs(l, x, n):
    acc = None
    for p in _split_bf16(x, n):
        t = _dot(l, p)
        acc = t if acc is None else acc + t
    return acc


def _inproj_kernel(h_ref, g_ref, w_ref, ws_ref, p_ref, small_ref, fft_ref, xn_ref):
    @pl.when(pl.program_id(1) == 0)
    def _():
        _rmsnorm_rows(h_ref, g_ref, xn_ref, INPROJ_BM)
        sm = _dot(xn_ref[...], ws_ref[...])
        small_ref[...] = sm
        fft_ref[...] = sm.T[SMALL_FF0:SMALL_FF0 + FOX_HEADS, :]

    p_ref[...] = _dot(xn_ref[...], w_ref[...]).astype(BF16)


def _inproj(l, h, g, w_main, w_small):
    bm, bn = INPROJ_BM, INPROJ_BN
    nbytes = (2 * bm * D_MODEL * 4 + 2 * D_MODEL * bn * 2 + 2 * D_MODEL * SMALL_W * 2
              + 2 * bm * bn * 2 + 2 * bm * SMALL_W * 4 + bm * D_MODEL * 2 + bm * bn * 4 + 2 * 8 * bm * 4)
    return pl.pallas_call(
        _inproj_kernel,
        grid=(M_ROWS // bm, N_PROJ // bn),
        in_specs=[
            pl.BlockSpec((bm, D_MODEL), lambda i, j: (i, 0)),
            _layer_spec(l, (1, D_MODEL), lambda i, j: (0, 0)),
            _layer_spec(l, (D_MODEL, bn), lambda i, j: (0, j)),
            _layer_spec(l, (D_MODEL, SMALL_W), lambda i, j: (0, 0)),
        ],
        out_specs=[
            pl.BlockSpec((bm, bn), lambda i, j: (i, j)),
            pl.BlockSpec((bm, SMALL_W), lambda i, j: (i, 0)),
            pl.BlockSpec((FOX_HEADS, bm), lambda i, j: (0, i)),
        ],
        out_shape=[
            jax.ShapeDtypeStruct((M_ROWS, N_PROJ), BF16),
            jax.ShapeDtypeStruct((M_ROWS, SMALL_W), F32),
            jax.ShapeDtypeStruct((FOX_HEADS, M_ROWS), F32),
        ],
        scratch_shapes=[pltpu.VMEM((bm, D_MODEL), BF16)],
        compiler_params=_params(("arbitrary", "arbitrary"), nbytes),
        name="inproj",
    )(h, g, w_main, w_small)


def _fox_cumsum_kernel(bf_ref, ff_ref, nc_ref, *, layer):
    li = lax.broadcasted_iota(jnp.int32, (LANES, LANES), 0)
    lj = lax.broadcasted_iota(jnp.int32, (LANES, LANES), 1)
    upper = jnp.where(li <= lj, 1.0, 0.0).astype(BF16)
    ones = jnp.ones((LANES, LANES), BF16)
    ri = lax.broadcasted_iota(jnp.int32, (CUM_ROWS, CUM_ROWS), 0)
    rj = lax.broadcasted_iota(jnp.int32, (CUM_ROWS, CUM_ROWS), 1)
    strict_lower = jnp.where(rj < ri, 1.0, 0.0).astype(BF16)
    for hd in range(FOX_HEADS):
        lf = _log_sigmoid(ff_ref[hd, 0] + bf_ref[layer, hd])
        c = _dot_f32_lhs(lf, upper, 3) + _dot_f32_rhs(strict_lower, _dot_f32_lhs(lf, ones, 3), 3)
        nc_ref[0, hd] = c * (-LOG2E)


def _fox_cumsum(l, b_forget, ff_rows):
    return pl.pallas_call(
        functools.partial(_fox_cumsum_kernel, layer=l),
        grid=(BATCH,),
        in_specs=[
            pl.BlockSpec(memory_space=pltpu.SMEM),
            pl.BlockSpec((FOX_HEADS, 1, CUM_ROWS, LANES), lambda b: (0, b, 0, 0)),
        ],
        out_specs=pl.BlockSpec((1, FOX_HEADS, CUM_ROWS, LANES), lambda b: (b, 0, 0, 0)),
        out_shape=jax.ShapeDtypeStruct((BATCH, FOX_HEADS, CUM_ROWS, LANES), F32),
        compiler_params=_params(("arbitrary",), 8 << 20),
        name="fox_cumsum",
    )(b_forget, ff_rows)


def _fox_kernel(q_ref, k_ref, v_ref, nc_ref, o_ref, vx_ref, s_ref, p_ref, m_ref, al_ref, acc_ref):
    hd_cols = [slice(hh * FOX_HEAD_DIM, (hh + 1) * FOX_HEAD_DIM) for hh in range(FOX_NH)]
    for hh in range(FOX_NH):
        vx_ref[hh, :, :FOX_HEAD_DIM] = v_ref[:, hd_cols[hh]]
        vx_ref[hh, :, FOX_HEAD_DIM:] = jnp.ones((T_ROWS, FOX_HEAD_DIM), BF16)
    lane_chunks = FOX_BK // LANES
    row_id = lax.broadcasted_iota(jnp.int32, (FOX_BQ, FOX_BK), 0)
    col_id = lax.broadcasted_iota(jnp.int32, (FOX_BQ, FOX_BK), 1)
    causal = col_id <= row_id

    def q_body(qi, _):
        q0 = pl.multiple_of(qi * FOX_BQ, FOX_BQ)
        m_ref[...] = jnp.full(m_ref.shape, MASK_VALUE, F32)
        acc_ref[...] = jnp.zeros(acc_ref.shape, F32)

        def tile(kj, diagonal):
            k0 = pl.multiple_of(kj * FOX_BK, FOX_BK)
            for hh in range(FOX_NH):
                s = _dot_nt(q_ref[pl.ds(q0, FOX_BQ), hd_cols[hh]], k_ref[pl.ds(k0, FOX_BK), hd_cols[hh]])
                s = s + jnp.concatenate(
                    [nc_ref[0, hh, pl.ds(kj * lane_chunks + c, 1), :] for c in range(lane_chunks)], axis=1)
                s_ref[hh] = jnp.where(causal, s, MASK_VALUE) if diagonal else s
            for hh in range(FOX_NH):
                for r in range(0, FOX_BQ, FOX_STRIP):
                    rows = slice(r, r + FOX_STRIP)
                    sv = s_ref[hh, rows, :]
                    m_old = m_ref[hh, rows, :]
                    m_new = jnp.maximum(m_old, jnp.max(sv, axis=1, keepdims=True))
                    m_ref[hh, rows, :] = m_new
                    al_ref[hh, rows, :] = jnp.exp2(m_old - m_new)
                    p_ref[hh, rows, :] = jnp.exp2(sv - jnp.concatenate([m_new] * lane_chunks, axis=1)).astype(BF16)
            for hh in range(FOX_NH):
                alpha = jnp.concatenate([al_ref[hh]] * 2, axis=1)
                acc_ref[hh] = alpha * acc_ref[hh] + _dot(p_ref[hh], vx_ref[hh, pl.ds(k0, FOX_BK), :])

        def kv_pair(pj, _):
            tile(2 * pj, False)
            tile(2 * pj + 1, False)
            return 0

        lax.fori_loop(0, qi // 2, kv_pair, 0)

        @pl.when(lax.rem(qi, 2) == 1)
        def _():
            tile(qi - 1, False)

        tile(qi, True)
        for hh in range(FOX_NH):
            acc = acc_ref[hh]
            o_ref[pl.ds(q0, FOX_BQ), hd_cols[hh]] = (
                acc[:, :FOX_HEAD_DIM] / acc[:, FOX_HEAD_DIM:]).astype(BF16)
        return 0

    lax.fori_loop(0, T_ROWS // FOX_BQ, q_body, 0)


def _fox_attention(proj, negc):
    w = FOX_NH * FOX_HEAD_DIM
    blk = (T_ROWS, w)
    nbytes = (2 * 4 * T_ROWS * w * 2 + FOX_NH * T_ROWS * 2 * FOX_HEAD_DIM * 2
              + FOX_NH * 6 * FOX_BQ * FOX_BK * 4 + 2 * FOX_NH * CUM_ROWS * LANES * 4)
    return pl.pallas_call(
        _fox_kernel,
        grid=(BATCH, FOX_HEADS // FOX_NH),
        in_specs=[
            pl.BlockSpec(blk, lambda b, g: (b, COL_FQ // w + g)),
            pl.BlockSpec(blk, lambda b, g: (b, COL_FK // w + g)),
            pl.BlockSpec(blk, lambda b, g: (b, COL_FV // w + g)),
            pl.BlockSpec((1, FOX_NH, CUM_ROWS, LANES), lambda b, g: (b, g, 0, 0)),
        ],
        out_specs=pl.BlockSpec(blk, lambda b, g: (b, g)),
        out_shape=jax.ShapeDtypeStruct((M_ROWS, FOX_WIDTH), BF16),
        scratch_shapes=[
            pltpu.VMEM((FOX_NH, T_ROWS, 2 * FOX_HEAD_DIM), BF16),
            pltpu.VMEM((FOX_NH, FOX_BQ, FOX_BK), F32),
            pltpu.VMEM((FOX_NH, FOX_BQ, FOX_BK), BF16),
            pltpu.VMEM((FOX_NH, FOX_BQ, LANES), F32),
            pltpu.VMEM((FOX_NH, FOX_BQ, LANES), F32),
            pltpu.VMEM((FOX_NH, FOX_BQ, 2 * FOX_HEAD_DIM), F32),
        ],
        compiler_params=_params(("arbitrary", "arbitrary"), nbytes),
        name="fox_attn",
    )(proj, proj, proj, negc)


def _gla_kernel(*refs):
    q_ref, k_ref = refs[0], refs[1]
    v_refs = refs[2:2 + GLA_HEADS]
    r_refs = refs[2 + GLA_HEADS:2 + 2 * GLA_HEADS]
    small_ref, wa_ref, ba_ref, gn_ref, o_ref, st_ref, b_ref = refs[2 + 2 * GLA_HEADS:]
    C = GLA_CHUNK
    log2_c = C.bit_length() - 1

    @pl.when(pl.program_id(1) == 0)
    def _():
        st_ref[...] = jnp.zeros_like(st_ref)

    z = _dot(small_ref[...].astype(BF16), wa_ref[...]) + ba_ref[...]
    g = _log_sigmoid(z) * (1.0 / GLA_TAU)
    ri = lax.broadcasted_iota(jnp.int32, (GLA_RB, GLA_RB), 0)
    rj = lax.broadcasted_iota(jnp.int32, (GLA_RB, GLA_RB), 1)
    same_chunk = jnp.right_shift(ri, log2_c) == jnp.right_shift(rj, log2_c)
    tri = jnp.where(same_chunk & (rj <= ri), 1.0, 0.0).astype(BF16)
    b_ref[...] = _dot_f32_rhs(tri, g, 2)

    ci = lax.broadcasted_iota(jnp.int32, (C, C), 0)
    cj = lax.broadcasted_iota(jnp.int32, (C, C), 1)
    tril = cj <= ci

    def chunk_body(c, _):
        r0 = pl.multiple_of(c * C, C)
        for hd in range(GLA_HEADS):
            ks = slice(hd * GLA_DK, (hd + 1) * GLA_DK)
            vs = slice(hd * GLA_DV, (hd + 1) * GLA_DV)
            b = b_ref[pl.ds(r0, C), ks]
            b_last = b[C - 1:C, :]
            qc = q_ref[pl.ds(r0, C), ks].astype(F32)
            kc = k_ref[pl.ds(r0, C), ks].astype(F32)
            vc = v_refs[hd][pl.ds(r0, C), :]
            q_dec = (qc * jnp.exp(b)).astype(BF16)
            k_inv = (kc * jnp.exp(-b)).astype(BF16)
            k_end = (kc * jnp.exp(b_last - b)).astype(BF16)
            a = jnp.where(tril, _dot_nt(q_dec, k_inv), 0.0).astype(BF16)
            st = st_ref[hd]
            o = _dot(a, vc) + _dot_nt(q_dec, st.astype(BF16))
            st_ref[hd] = st * jnp.exp(b_last) + _dot_tn(vc, k_end)
            rms = lax.rsqrt(jnp.mean(o * o, axis=-1, keepdims=True) + EPS)
            gr = r_refs[hd][pl.ds(r0, C), :].astype(F32)
            out = (o * rms) * gn_ref[:, vs] * (gr * _sigmoid(gr))
            o_ref[pl.ds(r0, C), vs] = out.astype(BF16)
        return 0

    lax.fori_loop(0, GLA_RB // C, chunk_body, 0)


def _gla(l, proj, small, wa_pad, b_alpha, gn_g):
    rb = GLA_RB
    nt = T_ROWS // rb
    row = lambda b, t: b * nt + t
    nbytes = (2 * rb * (2 * GLA_KEY_WIDTH + 2 * GLA_VAL_WIDTH) * 2 + 2 * rb * SMALL_W * 4
              + 2 * rb * GLA_VAL_WIDTH * 2 + GLA_HEADS * GLA_DV * GLA_DK * 4 + 4 * rb * GLA_KEY_WIDTH * 4
              + 2 * SMALL_W * GLA_KEY_WIDTH * 2)

    def head_specs(col0):
        return [pl.BlockSpec((rb, GLA_DV), functools.partial(lambda b, t, c: (row(b, t), c), c=col0 // GLA_DV + hd))
                for hd in range(GLA_HEADS)]

    return pl.pallas_call(
        _gla_kernel,
        grid=(BATCH, nt),
        in_specs=[
            pl.BlockSpec((rb, GLA_KEY_WIDTH), lambda b, t: (row(b, t), COL_GQ // GLA_KEY_WIDTH)),
            pl.BlockSpec((rb, GLA_KEY_WIDTH), lambda b, t: (row(b, t), COL_GK // GLA_KEY_WIDTH)),
            *head_specs(COL_GV),
            *head_specs(COL_GR),
            pl.BlockSpec((rb, SMALL_W), lambda b, t: (row(b, t), 0)),
            _layer_spec(l, (SMALL_W, GLA_KEY_WIDTH), lambda b, t: (0, 0)),
            _layer_spec(l, (1, GLA_KEY_WIDTH), lambda b, t: (0, 0)),
            _layer_spec(l, (1, GLA_VAL_WIDTH), lambda b, t: (0, 0)),
        ],
        out_specs=pl.BlockSpec((rb, GLA_VAL_WIDTH), lambda b, t: (row(b, t), 0)),
        out_shape=jax.ShapeDtypeStruct((M_ROWS, GLA_VAL_WIDTH), BF16),
        scratch_shapes=[
            pltpu.VMEM((GLA_HEADS, GLA_DV, GLA_DK), F32),
            pltpu.VMEM((rb, GLA_KEY_WIDTH), F32),
        ],
        compiler_params=_params(("arbitrary", "arbitrary"), nbytes),
        name="gla",
    )(*([proj] * (2 + 2 * GLA_HEADS)), small, wa_pad, b_alpha, gn_g)


def _merge_kernel(of_ref, og_ref, wf_ref, wg_ref, gf_ref, gg_ref, y_ref):
    t_fox = _dot(of_ref[...], wf_ref[...])
    t_gla = _dot(og_ref[...], wg_ref[...])
    y = _sigmoid(gf_ref[...].astype(F32)) * t_fox + _sigmoid(gg_ref[...].astype(F32)) * t_gla
    y_ref[...] = y.astype(BF16)


def _merge(l, o_fox, o_gla, w_o_fox, w_o_gla, proj):
    bm, bn = MERGE_BM, MERGE_BN
    nbytes = (2 * bm * (FOX_WIDTH + GLA_VAL_WIDTH) * 2 + 2 * (FOX_WIDTH + GLA_VAL_WIDTH) * bn * 2
              + 6 * bm * bn * 2 + 4 * bm * bn * 4)
    return pl.pallas_call(
        _merge_kernel,
        grid=(M_ROWS // bm, D_MODEL // bn),
        in_specs=[
            pl.BlockSpec((bm, FOX_WIDTH), lambda i, j: (i, 0)),
            pl.BlockSpec((bm, GLA_VAL_WIDTH), lambda i, j: (i, 0)),
            _layer_spec(l, (FOX_WIDTH, bn), lambda i, j: (0, j)),
            _layer_spec(l, (GLA_VAL_WIDTH, bn), lambda i, j: (0, j)),
            pl.BlockSpec((bm, bn), lambda i, j: (i, COL_GATE_FOX // bn + j)),
            pl.BlockSpec((bm, bn), lambda i, j: (i, COL_GATE_GLA // bn + j)),
        ],
        out_specs=pl.BlockSpec((bm, bn), lambda i, j: (i, j)),
        out_shape=jax.ShapeDtypeStruct((M_ROWS, D_MODEL), BF16),
        compiler_params=_params(("arbitrary", "arbitrary"), nbytes),
        name="merge",
    )(o_fox, o_gla, w_o_fox, w_o_gla, proj, proj)


def _outproj_kernel(y_ref, w_ref, h_ref, o_ref):
    o_ref[...] = h_ref[...] + _dot(y_ref[...], w_ref[...])


def _outproj(l, y, w_out, h):
    bm, bn = OUTPROJ_BM, OUTPROJ_BN
    nbytes = 2 * bm * D_MODEL * 2 + 2 * D_MODEL * bn * 2 + 4 * bm * bn * 4 + bm * bn * 4
    return pl.pallas_call(
        _outproj_kernel,
        grid=(M_ROWS // bm, D_MODEL // bn),
        in_specs=[
            pl.BlockSpec((bm, D_MODEL), lambda i, j: (i, 0)),
            _layer_spec(l, (D_MODEL, bn), lambda i, j: (0, j)),
            pl.BlockSpec((bm, bn), lambda i, j: (i, j)),
        ],
        out_specs=pl.BlockSpec((bm, bn), lambda i, j: (i, j)),
        out_shape=jax.ShapeDtypeStruct((M_ROWS, D_MODEL), F32),
        compiler_params=_params(("arbitrary", "arbitrary"), nbytes),
        name="outproj",
    )(y, w_out, h)


def _mlp_kernel(h_ref, g_ref, w1_ref, w2_ref, fg_ref, o_ref, xn_ref, *, final_norm):
    j = pl.program_id(1)

    @pl.when(j == 0)
    def _():
        _rmsnorm_rows(h_ref, g_ref, xn_ref, MLP_BM, MLP_NORM_CHUNK)
        o_ref[...] = h_ref[...]

    u = jnp.maximum(_dot(xn_ref[...], w1_ref[...]), 0.0)
    o_ref[...] += _dot((u * u).astype(BF16), w2_ref[...])

    if final_norm:
        @pl.when(j == pl.num_programs(1) - 1)
        def _():
            _rmsnorm_rows(o_ref, fg_ref, o_ref, MLP_BM, MLP_NORM_CHUNK)


def _mlp(l, h, g, w1, w2, final_g, final_norm):
    bm, bf = MLP_BM, MLP_BF
    nbytes = (4 * bm * D_MODEL * 4 + bm * D_MODEL * 2 + 4 * D_MODEL * bf * 2
              + bm * bf * 6 + bm * D_MODEL * 4)
    return pl.pallas_call(
        functools.partial(_mlp_kernel, final_norm=final_norm),
        grid=(M_ROWS // bm, D_FF // bf),
        in_specs=[
            pl.BlockSpec((bm, D_MODEL), lambda i, j: (i, 0)),
            _layer_spec(l, (1, D_MODEL), lambda i, j: (0, 0)),
            _layer_spec(l, (D_MODEL, bf), lambda i, j: (0, j)),
            _layer_spec(l, (bf, D_MODEL), lambda i, j: (j, 0)),
            pl.BlockSpec((1, D_MODEL), lambda i, j: (0, 0)),
        ],
        out_specs=pl.BlockSpec((bm, D_MODEL), lambda i, j: (i, 0)),
        out_shape=jax.ShapeDtypeStruct((M_ROWS, D_MODEL), F32),
        scratch_shapes=[pltpu.VMEM((bm, D_MODEL), BF16)],
        compiler_params=_params(("arbitrary", "arbitrary"), nbytes),
        name="mlp_final" if final_norm else "mlp",
    )(h, g, w1, w2, final_g)


def _wprep_kernel(w_ref, main_ref, small_ref):
    main_ref[:, COL_FQ:COL_FK] = (w_ref[:, :FOX_WIDTH] * (FOX_HEAD_DIM ** -0.5 * LOG2E)).astype(BF16)
    main_ref[:, COL_FK:COL_GQ] = w_ref[:, FOX_WIDTH:WIN_FF0].astype(BF16)
    seg_w = -(-(WIN_GA0 - WIN_FF0) // LANES) * LANES
    seg = w_ref[:, WIN_FF0:WIN_FF0 + seg_w]
    off = WIN_GQ0 - WIN_FF0
    main_ref[:, COL_GQ:COL_GK] = (seg[:, off:off + GLA_KEY_WIDTH] * (GLA_DK ** -0.5)).astype(BF16)
    main_ref[:, COL_GK:COL_GATE_FOX] = seg[:, off + GLA_KEY_WIDTH:off + (WIN_GA0 - WIN_GQ0)].astype(BF16)
    tail0 = (WIN_GA0 // LANES) * LANES
    tail = w_ref[:, tail0:]
    main_ref[:, COL_GATE_FOX:] = tail[:, WIN_GATE0 - tail0:].astype(BF16)
    rows = w_ref.shape[0]
    small_ref[...] = jnp.concatenate(
        [tail[:, WIN_GA0 - tail0:WIN_GATE0 - tail0], seg[:, :FOX_HEADS],
         jnp.zeros((rows, SMALL_W - GLA_RANK - FOX_HEADS), F32)], axis=1).astype(BF16)


def _prep_in_weights(w_in):
    rows = WPREP_ROWS
    n_in = w_in.shape[-1]
    nbytes = 2 * rows * n_in * 4 + 2 * rows * N_PROJ * 2 + rows * n_in * 4
    return pl.pallas_call(
        _wprep_kernel,
        grid=(DEPTH, D_MODEL // rows),
        in_specs=[pl.BlockSpec((None, rows, n_in), lambda l, i: (l, i, 0))],
        out_specs=[
            pl.BlockSpec((None, rows, N_PROJ), lambda l, i: (l, i, 0)),
            pl.BlockSpec((None, rows, SMALL_W), lambda l, i: (l, i, 0)),
        ],
        out_shape=[
            jax.ShapeDtypeStruct((DEPTH, D_MODEL, N_PROJ), BF16),
            jax.ShapeDtypeStruct((DEPTH, D_MODEL, SMALL_W), BF16),
        ],
        compiler_params=_params(("arbitrary", "arbitrary"), nbytes),
        name="wprep",
    )(w_in)


def kernel(x, meta_tokens, norm_mix_g, w_in, b_forget, w_alpha2, b_alpha, gla_norm_g, w_o_fox, w_o_gla,
           w_out, norm_mlp_g, w_ff1, w_ff2, final_norm_g):
    meta = jnp.broadcast_to(meta_tokens[None].astype(F32), (BATCH, N_META, D_MODEL))
    pad = jnp.zeros((BATCH, T_ROWS - N_META - SEQ, D_MODEL), F32)
    h = jnp.concatenate([meta, x.astype(F32), pad], axis=1).reshape(M_ROWS, D_MODEL)

    w_main, w_small = _prep_in_weights(w_in)
    wa_pad = jnp.pad(w_alpha2, ((0, 0), (0, SMALL_W - GLA_RANK), (0, 0))).astype(BF16)
    w_o_fox_b, w_o_gla_b, w_out_b = w_o_fox.astype(BF16), w_o_gla.astype(BF16), w_out.astype(BF16)
    w_ff1_b, w_ff2_b = w_ff1.astype(BF16), w_ff2.astype(BF16)
    norm_mix_g3 = norm_mix_g.reshape(DEPTH, 1, D_MODEL)
    norm_mlp_g3 = norm_mlp_g.reshape(DEPTH, 1, D_MODEL)
    b_alpha3 = b_alpha.reshape(DEPTH, 1, GLA_KEY_WIDTH)
    gla_norm_g3 = gla_norm_g.reshape(DEPTH, 1, GLA_VAL_WIDTH)
    final_g = final_norm_g.reshape(1, D_MODEL)

    for l in range(DEPTH):
        proj, small, fft = _inproj(l, h, norm_mix_g3, w_main, w_small)
        ff_rows = fft.reshape(FOX_HEADS, BATCH, T_LANE_ROWS, LANES)
        ff_rows = jnp.pad(ff_rows, ((0, 0), (0, 0), (0, CUM_ROWS - T_LANE_ROWS), (0, 0)))
        negc = _fox_cumsum(l, b_forget, ff_rows)
        o_fox = _fox_attention(proj, negc)
        o_gla = _gla(l, proj, small, wa_pad, b_alpha3, gla_norm_g3)
        y = _merge(l, o_fox, o_gla, w_o_fox_b, w_o_gla_b, proj)
        h = _outproj(l, y, w_out_b, h)
        h = _mlp(l, h, norm_mlp_g3, w_ff1_b, w_ff2_b, final_g, final_norm=(l == DEPTH - 1))

    return h.reshape(BATCH, T_ROWS, D_MODEL)[:, N_META:N_META + SEQ]
```

```python
import functools

import jax
import jax.numpy as jnp
from jax import lax
from jax.experimental import pallas as pl
from jax.experimental.pallas import tpu as pltpu

F32 = jnp.float32
BF16 = jnp.bfloat16

D_MODEL = 2048
BATCH = 4
SEQ = 4096
DEPTH = 2
N_META = 16
EPS = 1e-6
MASK_VALUE = -1e30
LOG2E = 1.4426950408889634

FOX_HEADS = 8
FOX_HEAD_DIM = 128
FOX_WIDTH = FOX_HEADS * FOX_HEAD_DIM
GLA_HEADS = 4
GLA_DK = 256
GLA_DV = 512
GLA_KEY_WIDTH = GLA_HEADS * GLA_DK
GLA_VAL_WIDTH = GLA_HEADS * GLA_DV
GLA_RANK = 16
GLA_TAU = 16.0
GLA_CHUNK = 128
GLA_PIVOT = 64
D_FF = 4 * D_MODEL

LANES = 128
V7X_VMEM_LIMIT_CAP = 60 * 1024 * 1024

T_ROWS = N_META + SEQ + (LANES - N_META)
M_ROWS = BATCH * T_ROWS
T_LANE_ROWS = T_ROWS // LANES
CUM_ROWS = 64

COL_FQ = 0
COL_FK = COL_FQ + FOX_WIDTH
COL_FV = COL_FK + FOX_WIDTH
COL_GQ = COL_FV + FOX_WIDTH
COL_GK = COL_GQ + GLA_KEY_WIDTH
COL_GV = COL_GK + GLA_KEY_WIDTH
COL_GR = COL_GV + GLA_VAL_WIDTH
COL_GATE_FOX = COL_GR + GLA_VAL_WIDTH
COL_GATE_GLA = COL_GATE_FOX + D_MODEL
N_PROJ = COL_GATE_GLA + D_MODEL
WIN_FF0 = 3 * FOX_WIDTH
WIN_GQ0 = WIN_FF0 + FOX_HEADS
WIN_GA0 = WIN_GQ0 + 2 * GLA_KEY_WIDTH + 2 * GLA_VAL_WIDTH
WIN_GATE0 = WIN_GA0 + GLA_RANK
SMALL_W = LANES
SMALL_FF0 = GLA_RANK

NORM_CHUNK = 128
INPROJ_BM, INPROJ_BN = 1408, 1024
MERGE_BM, MERGE_BN = 1408, 512
OUTPROJ_BM, OUTPROJ_BN = 1408, 1024
MLP_BM, MLP_BF = 1056, 1024
MLP_NORM_CHUNK = 176
FOX_NH = 4
FOX_BQ = 384
FOX_BK = 384
FOX_STRIP = 64
GLA_RB = 384
WPREP_ROWS = 128


def _vmem_limit(nbytes):
    return int(min(V7X_VMEM_LIMIT_CAP, nbytes + (4 << 20)))


def _params(sem, nbytes):
    return pltpu.CompilerParams(dimension_semantics=sem, vmem_limit_bytes=_vmem_limit(nbytes))


def _layer_spec(l, shape, index_map):
    return pl.BlockSpec((None,) + shape, lambda *idx: (l,) + index_map(*idx))


def _rmsnorm_rows(src_ref, g_ref, dst_ref, rows, chunk=NORM_CHUNK):
    g = g_ref[...]

    def body(c, _):
        r0 = pl.multiple_of(c * chunk, chunk)
        x = src_ref[pl.ds(r0, chunk), :]
        ms = jnp.mean(x * x, axis=-1, keepdims=True)
        dst_ref[pl.ds(r0, chunk), :] = ((x * lax.rsqrt(ms + EPS)) * g).astype(dst_ref.dtype)
        return 0

    lax.fori_loop(0, rows // chunk, body, 0)


def _log_sigmoid(x):
    return jnp.minimum(x, 0.0) - jnp.log(1.0 + jnp.exp(-jnp.abs(x)))


def _sigmoid(x):
    return 1.0 / (1.0 + jnp.exp(-x))


def _dot(a, b):
    return jnp.dot(a, b, preferred_element_type=F32)


def _dot_nt(a, b):
    return lax.dot_general(a, b, (((1,), (1,)), ((), ())), preferred_element_type=F32)


def _dot_tn(a, b):
    return lax.dot_general(a, b, (((0,), (0,)), ((), ())), preferred_element_type=F32)


def _split_bf16(x, n):
    pieces = []
    rem = x
    for _ in range(n):
        hi = rem.astype(BF16)
        pieces.append(hi)
        rem = rem - hi.astype(F32)
    return pieces


def _dot_f32_lhs(x, r, n):
    acc = None
    for p in _split_bf16(x, n):
        t = _dot(p, r)
        acc = t if acc is None else acc + t
    return acc


def _dot_f32_rhs(l, x, n):
    acc = None
    for p in _split_bf16(x, n):
        t = _dot(l, p)
        acc = t if acc is None else acc + t
    return acc


def _inproj_kernel(h_ref, g_ref, w_ref, ws_ref, p_ref, small_ref, fft_ref, xn_ref):
    @pl.when(pl.program_id(1) == 0)
    def _():
        _rmsnorm_rows(h_ref, g_ref, xn_ref, INPROJ_BM)
        sm = _dot(xn_ref[...], ws_ref[...])
        small_ref[...] = sm
        fft_ref[...] = sm.T[SMALL_FF0:SMALL_FF0 + FOX_HEADS, :]

    p_ref[...] = _dot(xn_ref[...], w_ref[...]).astype(BF16)


def _inproj(l, h, g, w_main, w_small):
    bm, bn = INPROJ_BM, INPROJ_BN
    nbytes = (2 * bm * D_MODEL * 4 + 2 * D_MODEL * bn * 2 + 2 * D_MODEL * SMALL_W * 2
              + 2 * bm * bn * 2 + 2 * bm * SMALL_W * 4 + bm * D_MODEL * 2 + bm * bn * 4 + 2 * 8 * bm * 4)
    return pl.pallas_call(
        _inproj_kernel,
        grid=(M_ROWS // bm, N_PROJ // bn),
        in_specs=[
            pl.BlockSpec((bm, D_MODEL), lambda i, j: (i, 0)),
            _layer_spec(l, (1, D_MODEL), lambda i, j: (0, 0)),
            _layer_spec(l, (D_MODEL, bn), lambda i, j: (0, j)),
            _layer_spec(l, (D_MODEL, SMALL_W), lambda i, j: (0, 0)),
        ],
        out_specs=[
            pl.BlockSpec((bm, bn), lambda i, j: (i, j)),
            pl.BlockSpec((bm, SMALL_W), lambda i, j: (i, 0)),
            pl.BlockSpec((FOX_HEADS, bm), lambda i, j: (0, i)),
        ],
        out_shape=[
            jax.ShapeDtypeStruct((M_ROWS, N_PROJ), BF16),
            jax.ShapeDtypeStruct((M_ROWS, SMALL_W), F32),
            jax.ShapeDtypeStruct((FOX_HEADS, M_ROWS), F32),
        ],
        scratch_shapes=[pltpu.VMEM((bm, D_MODEL), BF16)],
        compiler_params=_params(("arbitrary", "arbitrary"), nbytes),
        name="inproj",
    )(h, g, w_main, w_small)


def _fox_cumsum_kernel(bf_ref, ff_ref, nc_ref, *, layer):
    li = lax.broadcasted_iota(jnp.int32, (LANES, LANES), 0)
    lj = lax.broadcasted_iota(jnp.int32, (LANES, LANES), 1)
    upper = jnp.where(li <= lj, 1.0, 0.0).astype(BF16)
    ones = jnp.ones((LANES, LANES), BF16)
    ri = lax.broadcasted_iota(jnp.int32, (CUM_ROWS, CUM_ROWS), 0)
    rj = lax.broadcasted_iota(jnp.int32, (CUM_ROWS, CUM_ROWS), 1)
    strict_lower = jnp.where(rj < ri, 1.0, 0.0).astype(BF16)
    for hd in range(FOX_HEADS):
        lf = _log_sigmoid(ff_ref[hd, 0] + bf_ref[layer, hd])
        c = _dot_f32_lhs(lf, upper, 3) + _dot_f32_rhs(strict_lower, _dot_f32_lhs(lf, ones, 3), 3)
        nc_ref[0, hd] = c * (-LOG2E)


def _fox_cumsum(l, b_forget, ff_rows):
    return pl.pallas_call(
        functools.partial(_fox_cumsum_kernel, layer=l),
        grid=(BATCH,),
        in_specs=[
            pl.BlockSpec(memory_space=pltpu.SMEM),
            pl.BlockSpec((FOX_HEADS, 1, CUM_ROWS, LANES), lambda b: (0, b, 0, 0)),
        ],
        out_specs=pl.BlockSpec((1, FOX_HEADS, CUM_ROWS, LANES), lambda b: (b, 0, 0, 0)),
        out_shape=jax.ShapeDtypeStruct((BATCH, FOX_HEADS, CUM_ROWS, LANES), F32),
        compiler_params=_params(("arbitrary",), 8 << 20),
        name="fox_cumsum",
    )(b_forget, ff_rows)


def _fox_kernel(q_ref, k_ref, v_ref, nc_ref, o_ref, vx_ref, s_ref, p_ref, m_ref, al_ref, acc_ref):
    hd_cols = [slice(hh * FOX_HEAD_DIM, (hh + 1) * FOX_HEAD_DIM) for hh in range(FOX_NH)]
    for hh in range(FOX_NH):
        vx_ref[hh, :, :FOX_HEAD_DIM] = v_ref[:, hd_cols[hh]]
        vx_ref[hh, :, FOX_HEAD_DIM:] = jnp.ones((T_ROWS, FOX_HEAD_DIM), BF16)
    lane_chunks = FOX_BK // LANES
    row_id = lax.broadcasted_iota(jnp.int32, (FOX_BQ, FOX_BK), 0)
    col_id = lax.broadcasted_iota(jnp.int32, (FOX_BQ, FOX_BK), 1)
    causal = col_id <= row_id

    def q_body(qi, _):
        q0 = pl.multiple_of(qi * FOX_BQ, FOX_BQ)
        m_ref[...] = jnp.full(m_ref.shape, MASK_VALUE, F32)
        acc_ref[...] = jnp.zeros(acc_ref.shape, F32)

        def tile(kj, diagonal):
            k0 = pl.multiple_of(kj * FOX_BK, FOX_BK)
            for hh in range(FOX_NH):
                s = _dot_nt(q_ref[pl.ds(q0, FOX_BQ), hd_cols[hh]], k_ref[pl.ds(k0, FOX_BK), hd_cols[hh]])
                s = s + jnp.concatenate(
                    [nc_ref[0, hh, pl.ds(kj * lane_chunks + c, 1), :] for c in range(lane_chunks)], axis=1)
                s_ref[hh] = jnp.where(causal, s, MASK_VALUE) if diagonal else s
            for hh in range(FOX_NH):
                for r in range(0, FOX_BQ, FOX_STRIP):
                    rows = slice(r, r + FOX_STRIP)
                    sv = s_ref[hh, rows, :]
                    m_old = m_ref[hh, rows, :]
                    m_new = jnp.maximum(m_old, jnp.max(sv, axis=1, keepdims=True))
                    m_ref[hh, rows, :] = m_new
                    al_ref[hh, rows, :] = jnp.exp2(m_old - m_new)
                    p_ref[hh, rows, :] = jnp.exp2(sv - jnp.concatenate([m_new] * lane_chunks, axis=1)).astype(BF16)
            for hh in range(FOX_NH):
                alpha = jnp.concatenate([al_ref[hh]] * 2, axis=1)
                acc_ref[hh] = alpha * acc_ref[hh] + _dot(p_ref[hh], vx_ref[hh, pl.ds(k0, FOX_BK), :])

        def kv_pair(pj, _):
            tile(2 * pj, False)
            tile(2 * pj + 1, False)
            return 0

        lax.fori_loop(0, qi // 2, kv_pair, 0)

        @pl.when(lax.rem(qi, 2) == 1)
        def _():
            tile(qi - 1, False)

        tile(qi, True)
        for hh in range(FOX_NH):
            acc = acc_ref[hh]
            o_ref[pl.ds(q0, FOX_BQ), hd_cols[hh]] = (
                acc[:, :FOX_HEAD_DIM] / acc[:, FOX_HEAD_DIM:]).astype(BF16)
        return 0

    lax.fori_loop(0, T_ROWS // FOX_BQ, q_body, 0)


def _fox_attention(proj, negc):
    w = FOX_NH * FOX_HEAD_DIM
    blk = (T_ROWS, w)
    nbytes = (2 * 4 * T_ROWS * w * 2 + FOX_NH * T_ROWS * 2 * FOX_HEAD_DIM * 2
              + FOX_NH * 6 * FOX_BQ * FOX_BK * 4 + 2 * FOX_NH * CUM_ROWS * LANES * 4)
    return pl.pallas_call(
        _fox_kernel,
        grid=(BATCH, FOX_HEADS // FOX_NH),
        in_specs=[
            pl.BlockSpec(blk, lambda b, g: (b, COL_FQ // w + g)),
            pl.BlockSpec(blk, lambda b, g: (b, COL_FK // w + g)),
            pl.BlockSpec(blk, lambda b, g: (b, COL_FV // w + g)),
            pl.BlockSpec((1, FOX_NH, CUM_ROWS, LANES), lambda b, g: (b, g, 0, 0)),
        ],
        out_specs=pl.BlockSpec(blk, lambda b, g: (b, g)),
        out_shape=jax.ShapeDtypeStruct((M_ROWS, FOX_WIDTH), BF16),
        scratch_shapes=[
            pltpu.VMEM((FOX_NH, T_ROWS, 2 * FOX_HEAD_DIM), BF16),
            pltpu.VMEM((FOX_NH, FOX_BQ, FOX_BK), F32),
            pltpu.VMEM((FOX_NH, FOX_BQ, FOX_BK), BF16),
            pltpu.VMEM((FOX_NH, FOX_BQ, LANES), F32),
            pltpu.VMEM((FOX_NH, FOX_BQ, LANES), F32),
            pltpu.VMEM((FOX_NH, FOX_BQ, 2 * FOX_HEAD_DIM), F32),
        ],
        compiler_params=_params(("arbitrary", "arbitrary"), nbytes),
        name="fox_attn",
    )(proj, proj, proj, negc)


def _gla_kernel(*refs):
    q_ref, k_ref = refs[0], refs[1]
    v_refs = refs[2:2 + GLA_HEADS]
    r_refs = refs[2 + GLA_HEADS:2 + 2 * GLA_HEADS]
    small_ref, wa_ref, ba_ref, gn_ref, o_ref, st_ref, b_ref = refs[2 + 2 * GLA_HEADS:]
    C = GLA_CHUNK

    @pl.when(pl.program_id(1) == 0)
    def _():
        st_ref[...] = jnp.zeros_like(st_ref)

    ci = lax.broadcasted_iota(jnp.int32, (C, C), 0)
    cj = lax.broadcasted_iota(jnp.int32, (C, C), 1)
    tril = cj <= ci
    tril_ones = jnp.where(tril, 1.0, 0.0).astype(BF16)

    z = _dot(small_ref[...].astype(BF16), wa_ref[...]) + ba_ref[...]
    g = _log_sigmoid(z) * (LOG2E / GLA_TAU)
    for c in range(GLA_RB // C):
        b_ref[c * C:(c + 1) * C, :] = _dot_f32_rhs(tril_ones, g[c * C:(c + 1) * C, :], 2)

    def chunk_body(c, _):
        r0 = pl.multiple_of(c * C, C)
        for hd in range(GLA_HEADS):
            ks = slice(hd * GLA_DK, (hd + 1) * GLA_DK)
            vs = slice(hd * GLA_DV, (hd + 1) * GLA_DV)
            b = b_ref[pl.ds(r0, C), ks]
            b_mid = b[GLA_PIVOT - 1:GLA_PIVOT, :]
            b_last = b[C - 1:C, :]
            qc = q_ref[pl.ds(r0, C), ks].astype(F32)
            kc = k_ref[pl.ds(r0, C), ks].astype(F32)
            vc = v_refs[hd][pl.ds(r0, C), :]
            q_abs = (qc * jnp.exp2(b)).astype(BF16)
            q_dec = (qc * jnp.exp2(b - b_mid)).astype(BF16)
            k_inv = (kc * jnp.exp2(b_mid - b)).astype(BF16)
            k_end = (kc * jnp.exp2(b_last - b)).astype(BF16)
            a = jnp.where(tril, _dot_nt(q_dec, k_inv), 0.0).astype(BF16)
            st = st_ref[hd]
            o = _dot(a, vc) + _dot_nt(q_abs, st.astype(BF16))
            st_ref[hd] = st * jnp.exp2(b_last) + _dot_tn(vc, k_end)
            rms = lax.rsqrt(jnp.mean(o * o, axis=-1, keepdims=True) + EPS)
            gr = r_refs[hd][pl.ds(r0, C), :].astype(F32)
            out = (o * rms) * gn_ref[:, vs] * (gr * _sigmoid(gr))
            o_ref[pl.ds(r0, C), vs] = out.astype(BF16)
        return 0

    lax.fori_loop(0, GLA_RB // C, chunk_body, 0)


def _gla(l, proj, small, wa_pad, b_alpha, gn_g):
    rb = GLA_RB
    nt = T_ROWS // rb
    row = lambda b, t: b * nt + t
    nbytes = (2 * rb * (2 * GLA_KEY_WIDTH + 2 * GLA_VAL_WIDTH) * 2 + 2 * rb * SMALL_W * 4
              + 2 * rb * GLA_VAL_WIDTH * 2 + GLA_HEADS * GLA_DV * GLA_DK * 4 + 4 * rb * GLA_KEY_WIDTH * 4
              + 2 * SMALL_W * GLA_KEY_WIDTH * 2)

    def head_specs(col0):
        return [pl.BlockSpec((rb, GLA_DV), functools.partial(lambda b, t, c: (row(b, t), c), c=col0 // GLA_DV + hd))
                for hd in range(GLA_HEADS)]

    return pl.pallas_call(
        _gla_kernel,
        grid=(BATCH, nt),
        in_specs=[
            pl.BlockSpec((rb, GLA_KEY_WIDTH), lambda b, t: (row(b, t), COL_GQ // GLA_KEY_WIDTH)),
            pl.BlockSpec((rb, GLA_KEY_WIDTH), lambda b, t: (row(b, t), COL_GK // GLA_KEY_WIDTH)),
            *head_specs(COL_GV),
            *head_specs(COL_GR),
            pl.BlockSpec((rb, SMALL_W), lambda b, t: (row(b, t), 0)),
            _layer_spec(l, (SMALL_W, GLA_KEY_WIDTH), lambda b, t: (0, 0)),
            _layer_spec(l, (1, GLA_KEY_WIDTH), lambda b, t: (0, 0)),
            _layer_spec(l, (1, GLA_VAL_WIDTH), lambda b, t: (0, 0)),
        ],
        out_specs=pl.BlockSpec((rb, GLA_VAL_WIDTH), lambda b, t: (row(b, t), 0)),
        out_shape=jax.ShapeDtypeStruct((M_ROWS, GLA_VAL_WIDTH), BF16),
        scratch_shapes=[
            pltpu.VMEM((GLA_HEADS, GLA_DV, GLA_DK), F32),
            pltpu.VMEM((rb, GLA_KEY_WIDTH), F32),
        ],
        compiler_params=_params(("arbitrary", "arbitrary"), nbytes),
        name="gla",
    )(*([proj] * (2 + 2 * GLA_HEADS)), small, wa_pad, b_alpha, gn_g)


def _merge_kernel(of_ref, og_ref, wf_ref, wg_ref, gf_ref, gg_ref, y_ref):
    t_fox = _dot(of_ref[...], wf_ref[...])
    t_gla = _dot(og_ref[...], wg_ref[...])
    y = _sigmoid(gf_ref[...].astype(F32)) * t_fox + _sigmoid(gg_ref[...].astype(F32)) * t_gla
    y_ref[...] = y.astype(BF16)


def _merge(l, o_fox, o_gla, w_o_fox, w_o_gla, proj):
    bm, bn = MERGE_BM, MERGE_BN
    nbytes = (2 * bm * (FOX_WIDTH + GLA_VAL_WIDTH) * 2 + 2 * (FOX_WIDTH + GLA_VAL_WIDTH) * bn * 2
              + 6 * bm * bn * 2 + 4 * bm * bn * 4)
    return pl.pallas_call(
        _merge_kernel,
        grid=(M_ROWS // bm, D_MODEL // bn),
        in_specs=[
            pl.BlockSpec((bm, FOX_WIDTH), lambda i, j: (i, 0)),
            pl.BlockSpec((bm, GLA_VAL_WIDTH), lambda i, j: (i, 0)),
            _layer_spec(l, (FOX_WIDTH, bn), lambda i, j: (0, j)),
            _layer_spec(l, (GLA_VAL_WIDTH, bn), lambda i, j: (0, j)),
            pl.BlockSpec((bm, bn), lambda i, j: (i, COL_GATE_FOX // bn + j)),
            pl.BlockSpec((bm, bn), lambda i, j: (i, COL_GATE_GLA // bn + j)),
        ],
        out_specs=pl.BlockSpec((bm, bn), lambda i, j: (i, j)),
        out_shape=jax.ShapeDtypeStruct((M_ROWS, D_MODEL), BF16),
        compiler_params=_params(("arbitrary", "arbitrary"), nbytes),
        name="merge",
    )(o_fox, o_gla, w_o_fox, w_o_gla, proj, proj)


def _outproj_kernel(y_ref, w_ref, h_ref, o_ref):
    o_ref[...] = h_ref[...] + _dot(y_ref[...], w_ref[...])


def _outproj(l, y, w_out, h):
    bm, bn = OUTPROJ_BM, OUTPROJ_BN
    nbytes = 2 * bm * D_MODEL * 2 + 2 * D_MODEL * bn * 2 + 4 * bm * bn * 4 + bm * bn * 4
    return pl.pallas_call(
        _outproj_kernel,
        grid=(M_ROWS // bm, D_MODEL // bn),
        in_specs=[
            pl.BlockSpec((bm, D_MODEL), lambda i, j: (i, 0)),
            _layer_spec(l, (D_MODEL, bn), lambda i, j: (0, j)),
            pl.BlockSpec((bm, bn), lambda i, j: (i, j)),
        ],
        out_specs=pl.BlockSpec((bm, bn), lambda i, j: (i, j)),
        out_shape=jax.ShapeDtypeStruct((M_ROWS, D_MODEL), F32),
        compiler_params=_params(("arbitrary", "arbitrary"), nbytes),
        name="outproj",
    )(y, w_out, h)


def _mlp_kernel(h_ref, g_ref, w1_ref, w2_ref, fg_ref, o_ref, xn_ref, *, final_norm):
    j = pl.program_id(1)

    @pl.when(j == 0)
    def _():
        _rmsnorm_rows(h_ref, g_ref, xn_ref, MLP_BM, MLP_NORM_CHUNK)
        o_ref[...] = h_ref[...]

    u = jnp.maximum(_dot(xn_ref[...], w1_ref[...]), 0.0)
    o_ref[...] += _dot((u * u).astype(BF16), w2_ref[...])

    if final_norm:
        @pl.when(j == pl.num_programs(1) - 1)
        def _():
            _rmsnorm_rows(o_ref, fg_ref, o_ref, MLP_BM, MLP_NORM_CHUNK)


def _mlp(l, h, g, w1, w2, final_g, final_norm):
    bm, bf = MLP_BM, MLP_BF
    nbytes = (4 * bm * D_MODEL * 4 + bm * D_MODEL * 2 + 4 * D_MODEL * bf * 2
              + bm * bf * 6 + bm * D_MODEL * 4)
    return pl.pallas_call(
        functools.partial(_mlp_kernel, final_norm=final_norm),
        grid=(M_ROWS // bm, D_FF // bf),
        in_specs=[
            pl.BlockSpec((bm, D_MODEL), lambda i, j: (i, 0)),
            _layer_spec(l, (1, D_MODEL), lambda i, j: (0, 0)),
            _layer_spec(l, (D_MODEL, bf), lambda i, j: (0, j)),
            _layer_spec(l, (bf, D_MODEL), lambda i, j: (j, 0)),
            pl.BlockSpec((1, D_MODEL), lambda i, j: (0, 0)),
        ],
        out_specs=pl.BlockSpec((bm, D_MODEL), lambda i, j: (i, 0)),
        out_shape=jax.ShapeDtypeStruct((M_ROWS, D_MODEL), F32),
        scratch_shapes=[pltpu.VMEM((bm, D_MODEL), BF16)],
        compiler_params=_params(("arbitrary", "arbitrary"), nbytes),
        name="mlp_final" if final_norm else "mlp",
    )(h, g, w1, w2, final_g)


def _wprep_kernel(w_ref, main_ref, small_ref):
    main_ref[:, COL_FQ:COL_FK] = (w_ref[:, :FOX_WIDTH] * (FOX_HEAD_DIM ** -0.5 * LOG2E)).astype(BF16)
    main_ref[:, COL_FK:COL_GQ] = w_ref[:, FOX_WIDTH:WIN_FF0].astype(BF16)
    seg_w = -(-(WIN_GA0 - WIN_FF0) // LANES) * LANES
    seg = w_ref[:, WIN_FF0:WIN_FF0 + seg_w]
    off = WIN_GQ0 - WIN_FF0
    main_ref[:, COL_GQ:COL_GK] = (seg[:, off:off + GLA_KEY_WIDTH] * (GLA_DK ** -0.5)).astype(BF16)
    main_ref[:, COL_GK:COL_GATE_FOX] = seg[:, off + GLA_KEY_WIDTH:off + (WIN_GA0 - WIN_GQ0)].astype(BF16)
    tail0 = (WIN_GA0 // LANES) * LANES
    tail = w_ref[:, tail0:]
    main_ref[:, COL_GATE_FOX:] = tail[:, WIN_GATE0 - tail0:].astype(BF16)
    rows = w_ref.shape[0]
    small_ref[...] = jnp.concatenate(
        [tail[:, WIN_GA0 - tail0:WIN_GATE0 - tail0], seg[:, :FOX_HEADS],
         jnp.zeros((rows, SMALL_W - GLA_RANK - FOX_HEADS), F32)], axis=1).astype(BF16)


def _prep_in_weights(w_in):
    rows = WPREP_ROWS
    n_in = w_in.shape[-1]
    nbytes = 2 * rows * n_in * 4 + 2 * rows * N_PROJ * 2 + rows * n_in * 4
    return pl.pallas_call(
        _wprep_kernel,
        grid=(DEPTH, D_MODEL // rows),
        in_specs=[pl.BlockSpec((None, rows, n_in), lambda l, i: (l, i, 0))],
        out_specs=[
            pl.BlockSpec((None, rows, N_PROJ), lambda l, i: (l, i, 0)),
            pl.BlockSpec((None, rows, SMALL_W), lambda l, i: (l, i, 0)),
        ],
        out_shape=[
            jax.ShapeDtypeStruct((DEPTH, D_MODEL, N_PROJ), BF16),
            jax.ShapeDtypeStruct((DEPTH, D_MODEL, SMALL_W), BF16),
        ],
        compiler_params=_params(("arbitrary", "arbitrary"), nbytes),
        name="wprep",
    )(w_in)


def kernel(x, meta_tokens, norm_mix_g, w_in, b_forget, w_alpha2, b_alpha, gla_norm_g, w_o_fox, w_o_gla,
           w_out, norm_mlp_g, w_ff1, w_ff2, final_norm_g):
    meta = jnp.broadcast_to(meta_tokens[None].astype(F32), (BATCH, N_META, D_MODEL))
    pad = jnp.zeros((BATCH, T_ROWS - N_META - SEQ, D_MODEL), F32)
    h = jnp.concatenate([meta, x.astype(F32), pad], axis=1).reshape(M_ROWS, D_MODEL)

    w_main, w_small = _prep_in_weights(w_in)
    wa_pad = jnp.pad(w_alpha2, ((0, 0), (0, SMALL_W - GLA_RANK), (0, 0))).astype(BF16)
    w_o_fox_b, w_o_gla_b, w_out_b = w_o_fox.astype(BF16), w_o_gla.astype(BF16), w_out.astype(BF16)
    w_ff1_b, w_ff2_b = w_ff1.astype(BF16), w_ff2.astype(BF16)
    norm_mix_g3 = norm_mix_g.reshape(DEPTH, 1, D_MODEL)
    norm_mlp_g3 = norm_mlp_g.reshape(DEPTH, 1, D_MODEL)
    b_alpha3 = b_alpha.reshape(DEPTH, 1, GLA_KEY_WIDTH)
    gla_norm_g3 = gla_norm_g.reshape(DEPTH, 1, GLA_VAL_WIDTH)
    final_g = final_norm_g.reshape(1, D_MODEL)

    for l in range(DEPTH):
        proj, small, fft = _inproj(l, h, norm_mix_g3, w_main, w_small)
        ff_rows = fft.reshape(FOX_HEADS, BATCH, T_LANE_ROWS, LANES)
        ff_rows = jnp.pad(ff_rows, ((0, 0), (0, 0), (0, CUM_ROWS - T_LANE_ROWS), (0, 0)))
        negc = _fox_cumsum(l, b_forget, ff_rows)
        o_fox = _fox_attention(proj, negc)
        o_gla = _gla(l, proj, small, wa_pad, b_alpha3, gla_norm_g3)
        y = _merge(l, o_fox, o_gla, w_o_fox_b, w_o_gla_b, proj)
        h = _outproj(l, y, w_out_b, h)
        h = _mlp(l, h, norm_mlp_g3, w_ff1_b, w_ff2_b, final_g, final_norm=(l == DEPTH - 1))

    return h.reshape(BATCH, T_ROWS, D_MODEL)[:, N_META:N_META + SEQ]
```

```python
import functools

import jax
import jax.numpy as jnp
from jax import lax
from jax.experimental import pallas as pl
from jax.experimental.pallas import tpu as pltpu

F32 = jnp.float32
BF16 = jnp.bfloat16

D_MODEL = 2048
BATCH = 4
SEQ = 4096
DEPTH = 2
N_META = 16
EPS = 1e-6
MASK_VALUE = -1e30
LOG2E = 1.4426950408889634

FOX_HEADS = 8
FOX_HEAD_DIM = 128
FOX_WIDTH = FOX_HEADS * FOX_HEAD_DIM
GLA_HEADS = 4
GLA_DK = 256
GLA_DV = 512
GLA_KEY_WIDTH = GLA_HEADS * GLA_DK
GLA_VAL_WIDTH = GLA_HEADS * GLA_DV
GLA_RANK = 16
GLA_TAU = 16.0
GLA_CHUNK = 128
GLA_PIVOT = 64
D_FF = 4 * D_MODEL

LANES = 128
SUBLANES = 8
V7X_VMEM_LIMIT_CAP = 60 * 1024 * 1024

MX = BATCH * SEQ
MM = LANES
SEQ_LANE_ROWS = SEQ // LANES
NC_META_ROW = SEQ_LANE_ROWS
NC_ROWS = SEQ_LANE_ROWS + SUBLANES

COL_FQ = 0
COL_FK = COL_FQ + FOX_WIDTH
COL_FV = COL_FK + FOX_WIDTH
COL_GQ = COL_FV + FOX_WIDTH
COL_GK = COL_GQ + GLA_KEY_WIDTH
COL_GV = COL_GK + GLA_KEY_WIDTH
COL_GR = COL_GV + GLA_VAL_WIDTH
COL_GATE_FOX = COL_GR + GLA_VAL_WIDTH
COL_GATE_GLA = COL_GATE_FOX + D_MODEL
N_PROJ = COL_GATE_GLA + D_MODEL
WIN_FF0 = 3 * FOX_WIDTH
WIN_GQ0 = WIN_FF0 + FOX_HEADS
WIN_GA0 = WIN_GQ0 + 2 * GLA_KEY_WIDTH + 2 * GLA_VAL_WIDTH
WIN_GATE0 = WIN_GA0 + GLA_RANK
SMALL_W = LANES
SMALL_FF0 = GLA_RANK

NORM_CHUNK = 128
INPROJ_BM, INPROJ_BN = 1024, 1664
MERGE_BM, MERGE_BN = 2048, 512
OUTPROJ_BM, OUTPROJ_BN = 1024, 1024
MLP_BM, MLP_BF = 1024, 512
FOX_NH = 4
FOX_BQ = 512
FOX_BK = 512
FOX_STRIP = 64
GLA_RB = 512
WPREP_ROWS = 512
WPREP_TAIL = 32


def _vmem_limit(nbytes):
    return int(min(V7X_VMEM_LIMIT_CAP, nbytes + (4 << 20)))


def _params(sem, nbytes):
    return pltpu.CompilerParams(dimension_semantics=sem, vmem_limit_bytes=_vmem_limit(nbytes))


def _layer_spec(l, shape, index_map):
    return pl.BlockSpec((None,) + shape, lambda *idx: (l,) + index_map(*idx))


def _meta_col(nj):
    return lambda i, j: (0, jnp.where(i == 0, j, nj - 1))


def _rmsnorm_rows(src_ref, g_ref, dst_ref, rows, chunk=NORM_CHUNK):
    g = g_ref[...]

    def body(c, _):
        r0 = pl.multiple_of(c * chunk, chunk)
        x = src_ref[pl.ds(r0, chunk), :]
        ms = jnp.mean(x * x, axis=-1, keepdims=True)
        dst_ref[pl.ds(r0, chunk), :] = ((x * lax.rsqrt(ms + EPS)) * g).astype(dst_ref.dtype)
        return 0

    lax.fori_loop(0, rows // chunk, body, 0)


def _log_sigmoid(x):
    return jnp.minimum(x, 0.0) - jnp.log(1.0 + jnp.exp(-jnp.abs(x)))


def _sigmoid(x):
    return 1.0 / (1.0 + jnp.exp(-x))


def _dot(a, b):
    return jnp.dot(a, b, preferred_element_type=F32)


def _dot_nt(a, b):
    return lax.dot_general(a, b, (((1,), (1,)), ((), ())), preferred_element_type=F32)


def _dot_tn(a, b):
    return lax.dot_general(a, b, (((0,), (0,)), ((), ())), preferred_element_type=F32)


def _split_bf16(x, n):
    pieces = []
    rem = x
    for _ in range(n):
        hi = rem.astype(BF16)
        pieces.append(hi)
        rem = rem - hi.astype(F32)
    return pieces


def _dot_f32_lhs(x, r, n):
    acc = None
    for p in _split_bf16(x, n):
        t = _dot(p, r)
        acc = t if acc is None else acc + t
    return acc


def _dot_f32_rhs(l, x, n):
    acc = None
    for p in _split_bf16(x, n):
        t = _dot(l, p)
        acc = t if acc is None else acc + t
    return acc


def _is_first_row_block():
    return pl.program_id(0) == 0


def _inproj_kernel(hx_ref, hm_ref, g_ref, w_ref, ws_ref, w1_ref, w2_ref,
                   px_ref, smx_ref, ffx_ref, pm_ref, smm_ref, ffm_ref, w1b_ref, w2b_ref, xn_ref, xnm_ref):
    j = pl.program_id(1)
    w1b_ref[...] = w1_ref[...].astype(BF16)
    w2b_ref[...] = w2_ref[...].astype(BF16)
    ff_rows = slice(SMALL_FF0, SMALL_FF0 + FOX_HEADS)

    @pl.when(j == 0)
    def _():
        _rmsnorm_rows(hx_ref, g_ref, xn_ref, INPROJ_BM)
        sm = _dot(xn_ref[...], ws_ref[...])
        smx_ref[...] = sm
        ffx_ref[...] = sm.T[ff_rows, :]

    @pl.when(jnp.logical_and(_is_first_row_block(), j == 0))
    def _():
        _rmsnorm_rows(hm_ref, g_ref, xnm_ref, MM)
        sm = _dot(xnm_ref[...], ws_ref[...])
        smm_ref[...] = sm
        ffm_ref[...] = sm.T[ff_rows, :]

    px_ref[...] = _dot(xn_ref[...], w_ref[...]).astype(BF16)

    @pl.when(_is_first_row_block())
    def _():
        pm_ref[...] = _dot(xnm_ref[...], w_ref[...]).astype(BF16)


def _inproj(l, hx, hm, g, w_main, w_small, w_ff1, w_ff2):
    bm, bn = INPROJ_BM, INPROJ_BN
    nj = N_PROJ // bn
    steps = (MX // bm) * nj
    r1, r2 = D_MODEL // steps, D_FF // steps
    slab = lambda i, j: (i * nj + j, 0)
    nbytes = (2 * bm * D_MODEL * 4 + 2 * D_MODEL * bn * 2 + 2 * D_MODEL * SMALL_W * 2
              + 2 * bm * bn * 2 + 2 * bm * SMALL_W * 4 + bm * D_MODEL * 2 + bm * bn * 4 + 2 * 8 * bm * 4
              + 2 * MM * D_MODEL * 4 + 2 * MM * bn * 2 + MM * D_MODEL * 2)
    return pl.pallas_call(
        _inproj_kernel,
        grid=(MX // bm, nj),
        in_specs=[
            pl.BlockSpec((bm, D_MODEL), lambda i, j: (i, 0)),
            pl.BlockSpec((MM, D_MODEL), lambda i, j: (0, 0)),
            _layer_spec(l, (1, D_MODEL), lambda i, j: (0, 0)),
            _layer_spec(l, (D_MODEL, bn), lambda i, j: (0, j)),
            _layer_spec(l, (D_MODEL, SMALL_W), lambda i, j: (0, 0)),
            _layer_spec(l, (r1, D_FF), slab),
            _layer_spec(l, (r2, D_MODEL), slab),
        ],
        out_specs=[
            pl.BlockSpec((bm, bn), lambda i, j: (i, j)),
            pl.BlockSpec((bm, SMALL_W), lambda i, j: (i, 0)),
            pl.BlockSpec((FOX_HEADS, bm), lambda i, j: (0, i)),
            pl.BlockSpec((MM, bn), _meta_col(nj)),
            pl.BlockSpec((MM, SMALL_W), lambda i, j: (0, 0)),
            pl.BlockSpec((FOX_HEADS, MM), lambda i, j: (0, 0)),
            pl.BlockSpec((r1, D_FF), slab),
            pl.BlockSpec((r2, D_MODEL), slab),
        ],
        out_shape=[
            jax.ShapeDtypeStruct((MX, N_PROJ), BF16),
            jax.ShapeDtypeStruct((MX, SMALL_W), F32),
            jax.ShapeDtypeStruct((FOX_HEADS, MX), F32),
            jax.ShapeDtypeStruct((MM, N_PROJ), BF16),
            jax.ShapeDtypeStruct((MM, SMALL_W), F32),
            jax.ShapeDtypeStruct((FOX_HEADS, MM), F32),
            jax.ShapeDtypeStruct((D_MODEL, D_FF), BF16),
            jax.ShapeDtypeStruct((D_FF, D_MODEL), BF16),
        ],
        scratch_shapes=[pltpu.VMEM((bm, D_MODEL), BF16), pltpu.VMEM((MM, D_MODEL), BF16)],
        compiler_params=_params(("arbitrary", "arbitrary"), nbytes + 12 * r1 * D_FF + 12 * r2 * D_MODEL),
        name="inproj",
    )(hx, hm, g, w_main, w_small, w_ff1, w_ff2)


def _fox_cumsum_kernel(bf_ref, ffx_ref, ffm_ref, nc_ref, *, layer):
    li = lax.broadcasted_iota(jnp.int32, (LANES, LANES), 0)
    lj = lax.broadcasted_iota(jnp.int32, (LANES, LANES), 1)
    upper = jnp.where(li <= lj, 1.0, 0.0).astype(BF16)
    ones = jnp.ones((LANES, LANES), BF16)
    ri = lax.broadcasted_iota(jnp.int32, (SEQ_LANE_ROWS, SEQ_LANE_ROWS), 0)
    rj = lax.broadcasted_iota(jnp.int32, (SEQ_LANE_ROWS, SEQ_LANE_ROWS), 1)
    strict_lower = jnp.where(rj < ri, 1.0, 0.0).astype(BF16)
    head_id = lax.broadcasted_iota(jnp.int32, (FOX_HEADS, LANES), 0)
    lane_id = lax.broadcasted_iota(jnp.int32, (FOX_HEADS, LANES), 1)
    bias = jnp.zeros((FOX_HEADS, LANES), F32)
    for hd in range(FOX_HEADS):
        bias = jnp.where(head_id == hd, bf_ref[layer, hd], bias)
    real = lane_id < N_META
    lf_m = jnp.where(real, _log_sigmoid(ffm_ref[...] + bias), 0.0)
    cum_m = _dot_f32_lhs(lf_m, upper, 3)
    tot_m = _dot_f32_lhs(lf_m, ones, 3)
    nc_meta = jnp.where(real, cum_m * (-LOG2E), MASK_VALUE)
    group_row = lax.broadcasted_iota(jnp.int32, (SUBLANES, LANES), 0)
    for hd in range(FOX_HEADS):
        lf = _log_sigmoid(ffx_ref[hd, 0] + bf_ref[layer, hd])
        c = (_dot_f32_lhs(lf, upper, 3) + _dot_f32_rhs(strict_lower, _dot_f32_lhs(lf, ones, 3), 3)
             + tot_m[hd:hd + 1, :])
        nc_ref[0, hd, :SEQ_LANE_ROWS, :] = c * (-LOG2E)
        nc_ref[0, hd, SEQ_LANE_ROWS:, :] = jnp.where(
            group_row == 0, jnp.broadcast_to(nc_meta[hd:hd + 1, :], (SUBLANES, LANES)), 0.0)


def _fox_cumsum(l, b_forget, ffx_rows, ffm):
    return pl.pallas_call(
        functools.partial(_fox_cumsum_kernel, layer=l),
        grid=(BATCH,),
        in_specs=[
            pl.BlockSpec(memory_space=pltpu.SMEM),
            pl.BlockSpec((FOX_HEADS, 1, SEQ_LANE_ROWS, LANES), lambda b: (0, b, 0, 0)),
            pl.BlockSpec((FOX_HEADS, MM), lambda b: (0, 0)),
        ],
        out_specs=pl.BlockSpec((1, FOX_HEADS, NC_ROWS, LANES), lambda b: (b, 0, 0, 0)),
        out_shape=jax.ShapeDtypeStruct((BATCH, FOX_HEADS, NC_ROWS, LANES), F32),
        compiler_params=_params(("arbitrary",), 8 << 20),
        name="fox_cumsum",
    )(b_forget, ffx_rows, ffm)


def _fox_kernel(qx_ref, kx_ref, vx_ref, qm_ref, km_ref, vm_ref, nc_ref, ox_ref, om_ref,
                vxs_ref, vms_ref, s_ref, p_ref, m_ref, al_ref, acc_ref):
    hd_cols = [slice(hh * FOX_HEAD_DIM, (hh + 1) * FOX_HEAD_DIM) for hh in range(FOX_NH)]
    for hh in range(FOX_NH):
        vxs_ref[hh, :, :FOX_HEAD_DIM] = vx_ref[:, hd_cols[hh]]
        vxs_ref[hh, :, FOX_HEAD_DIM:] = jnp.ones((SEQ, FOX_HEAD_DIM), BF16)
        vms_ref[hh, :, :FOX_HEAD_DIM] = vm_ref[:, hd_cols[hh]]
        vms_ref[hh, :, FOX_HEAD_DIM:] = jnp.ones((MM, FOX_HEAD_DIM), BF16)
    lane_chunks = FOX_BK // LANES
    row_id = lax.broadcasted_iota(jnp.int32, (FOX_BQ, FOX_BK), 0)
    col_id = lax.broadcasted_iota(jnp.int32, (FOX_BQ, FOX_BK), 1)
    causal = col_id <= row_id

    def meta_bias(hh):
        return nc_ref[0, hh, NC_META_ROW:NC_META_ROW + 1, :]

    for hh in range(FOX_NH):
        s = _dot_nt(qm_ref[:, hd_cols[hh]], km_ref[:, hd_cols[hh]]) + meta_bias(hh)
        s = jnp.where(causal[:MM, :MM], s, MASK_VALUE)
        p = jnp.exp2(s - jnp.max(s, axis=1, keepdims=True)).astype(BF16)
        acc = _dot(p, vms_ref[hh])
        om_ref[:, hd_cols[hh]] = (acc[:, :FOX_HEAD_DIM] / acc[:, FOX_HEAD_DIM:]).astype(BF16)

    def q_body(qi, _):
        q0 = pl.multiple_of(qi * FOX_BQ, FOX_BQ)
        for hh in range(FOX_NH):
            s = _dot_nt(qx_ref[pl.ds(q0, FOX_BQ), hd_cols[hh]], km_ref[:, hd_cols[hh]]) + meta_bias(hh)
            m0 = jnp.max(s, axis=1, keepdims=True)
            m_ref[hh] = jnp.broadcast_to(m0, (FOX_BQ, LANES))
            acc_ref[hh] = _dot(jnp.exp2(s - m0).astype(BF16), vms_ref[hh])

        def tile(kj, diagonal):
            k0 = pl.multiple_of(kj * FOX_BK, FOX_BK)
            for hh in range(FOX_NH):
                s = _dot_nt(qx_ref[pl.ds(q0, FOX_BQ), hd_cols[hh]], kx_ref[pl.ds(k0, FOX_BK), hd_cols[hh]])
                s = s + jnp.concatenate(
                    [nc_ref[0, hh, pl.ds(kj * lane_chunks + c, 1), :] for c in range(lane_chunks)], axis=1)
                s_ref[hh] = jnp.where(causal, s, MASK_VALUE) if diagonal else s
            for hh in range(FOX_NH):
                for r in range(0, FOX_BQ, FOX_STRIP):
                    rows = slice(r, r + FOX_STRIP)
                    sv = s_ref[hh, rows, :]
                    m_old = m_ref[hh, rows, :]
                    m_new = jnp.maximum(m_old, jnp.max(sv, axis=1, keepdims=True))
                    m_ref[hh, rows, :] = m_new
                    al_ref[hh, rows, :] = jnp.exp2(m_old - m_new)
                    p_ref[hh, rows, :] = jnp.exp2(sv - jnp.concatenate([m_new] * lane_chunks, axis=1)).astype(BF16)
            for hh in range(FOX_NH):
                alpha = jnp.concatenate([al_ref[hh]] * 2, axis=1)
                acc_ref[hh] = alpha * acc_ref[hh] + _dot(p_ref[hh], vxs_ref[hh, pl.ds(k0, FOX_BK), :])

        def kv_pair(pj, _):
            tile(2 * pj, False)
            tile(2 * pj + 1, False)
            return 0

        lax.fori_loop(0, qi // 2, kv_pair, 0)

        @pl.when(lax.rem(qi, 2) == 1)
        def _():
            tile(qi - 1, False)

        tile(qi, True)
        for hh in range(FOX_NH):
            acc = acc_ref[hh]
            ox_ref[pl.ds(q0, FOX_BQ), hd_cols[hh]] = (
                acc[:, :FOX_HEAD_DIM] / acc[:, FOX_HEAD_DIM:]).astype(BF16)
        return 0

    lax.fori_loop(0, SEQ // FOX_BQ, q_body, 0)


def _fox_attention(px, pm, negc):
    w = FOX_NH * FOX_HEAD_DIM
    xblk, mblk = (SEQ, w), (MM, w)
    nbytes = (2 * 4 * SEQ * w * 2 + 2 * 4 * MM * w * 2 + FOX_NH * (SEQ + MM) * 2 * FOX_HEAD_DIM * 2
              + FOX_NH * FOX_BQ * (FOX_BK * 6 + 2 * LANES * 4 + 2 * FOX_HEAD_DIM * 4)
              + 2 * FOX_NH * NC_ROWS * LANES * 4)
    return pl.pallas_call(
        _fox_kernel,
        grid=(BATCH, FOX_HEADS // FOX_NH),
        in_specs=[
            pl.BlockSpec(xblk, lambda b, g: (b, COL_FQ // w + g)),
            pl.BlockSpec(xblk, lambda b, g: (b, COL_FK // w + g)),
            pl.BlockSpec(xblk, lambda b, g: (b, COL_FV // w + g)),
            pl.BlockSpec(mblk, lambda b, g: (0, COL_FQ // w + g)),
            pl.BlockSpec(mblk, lambda b, g: (0, COL_FK // w + g)),
            pl.BlockSpec(mblk, lambda b, g: (0, COL_FV // w + g)),
            pl.BlockSpec((1, FOX_NH, NC_ROWS, LANES), lambda b, g: (b, g, 0, 0)),
        ],
        out_specs=[pl.BlockSpec(xblk, lambda b, g: (b, g)), pl.BlockSpec((None,) + mblk, lambda b, g: (b, 0, g))],
        out_shape=[jax.ShapeDtypeStruct((MX, FOX_WIDTH), BF16),
                   jax.ShapeDtypeStruct((BATCH, MM, FOX_WIDTH), BF16)],
        scratch_shapes=[
            pltpu.VMEM((FOX_NH, SEQ, 2 * FOX_HEAD_DIM), BF16),
            pltpu.VMEM((FOX_NH, MM, 2 * FOX_HEAD_DIM), BF16),
            pltpu.VMEM((FOX_NH, FOX_BQ, FOX_BK), F32),
            pltpu.VMEM((FOX_NH, FOX_BQ, FOX_BK), BF16),
            pltpu.VMEM((FOX_NH, FOX_BQ, LANES), F32),
            pltpu.VMEM((FOX_NH, FOX_BQ, LANES), F32),
            pltpu.VMEM((FOX_NH, FOX_BQ, 2 * FOX_HEAD_DIM), F32),
        ],
        compiler_params=_params(("arbitrary", "arbitrary"), nbytes),
        name="fox_attn",
    )(px, px, px, pm, pm, pm, negc)


def _gla_kernel(*refs):
    nh = GLA_HEADS
    qx_ref, kx_ref = refs[0], refs[1]
    vx_refs, rx_refs = refs[2:2 + nh], refs[2 + nh:2 + 2 * nh]
    smx_ref = refs[2 + 2 * nh]
    qm_ref, km_ref = refs[3 + 2 * nh], refs[4 + 2 * nh]
    vm_refs, rm_refs = refs[5 + 2 * nh:5 + 3 * nh], refs[5 + 3 * nh:5 + 4 * nh]
    smm_ref, wa_ref, ba_ref, gn_ref, ox_ref, om_ref, st_ref, b_ref, bm_ref = refs[5 + 4 * nh:]
    C = GLA_CHUNK

    ci = lax.broadcasted_iota(jnp.int32, (C, C), 0)
    cj = lax.broadcasted_iota(jnp.int32, (C, C), 1)
    tril = cj <= ci
    tril_ones = jnp.where(tril, 1.0, 0.0).astype(BF16)

    def log2_decay(small):
        z = _dot(small.astype(BF16), wa_ref[...]) + ba_ref[...]
        return _log_sigmoid(z) * (LOG2E / GLA_TAU)

    def chunk(hd, b, qc, kc, vc, gr):
        vs = slice(hd * GLA_DV, (hd + 1) * GLA_DV)
        b_mid = b[GLA_PIVOT - 1:GLA_PIVOT, :]
        b_last = b[C - 1:C, :]
        q_abs = (qc * jnp.exp2(b)).astype(BF16)
        q_dec = (qc * jnp.exp2(b - b_mid)).astype(BF16)
        k_inv = (kc * jnp.exp2(b_mid - b)).astype(BF16)
        k_end = (kc * jnp.exp2(b_last - b)).astype(BF16)
        a = jnp.where(tril, _dot_nt(q_dec, k_inv), 0.0).astype(BF16)
        st = st_ref[hd]
        o = _dot(a, vc) + _dot_nt(q_abs, st.astype(BF16))
        st_ref[hd] = st * jnp.exp2(b_last) + _dot_tn(vc, k_end)
        rms = lax.rsqrt(jnp.mean(o * o, axis=-1, keepdims=True) + EPS)
        return ((o * rms) * gn_ref[:, vs] * (gr * _sigmoid(gr))).astype(BF16)

    @pl.when(pl.program_id(1) == 0)
    def _():
        st_ref[...] = jnp.zeros_like(st_ref)
        real = lax.broadcasted_iota(jnp.int32, (MM, 1), 0) < N_META
        bm_ref[...] = _dot_f32_rhs(tril_ones, jnp.where(real, log2_decay(smm_ref[...]), 0.0), 2)
        for hd in range(nh):
            ks = slice(hd * GLA_DK, (hd + 1) * GLA_DK)
            vs = slice(hd * GLA_DV, (hd + 1) * GLA_DV)
            kc = jnp.where(real, km_ref[:, ks].astype(F32), 0.0)
            om_ref[:, vs] = chunk(hd, bm_ref[:, ks], qm_ref[:, ks].astype(F32), kc, vm_refs[hd][...],
                                  rm_refs[hd][...].astype(F32))

    g = log2_decay(smx_ref[...])
    for c in range(GLA_RB // C):
        b_ref[c * C:(c + 1) * C, :] = _dot_f32_rhs(tril_ones, g[c * C:(c + 1) * C, :], 2)

    def chunk_body(c, _):
        r0 = pl.multiple_of(c * C, C)
        for hd in range(nh):
            ks = slice(hd * GLA_DK, (hd + 1) * GLA_DK)
            vs = slice(hd * GLA_DV, (hd + 1) * GLA_DV)
            ox_ref[pl.ds(r0, C), vs] = chunk(
                hd, b_ref[pl.ds(r0, C), ks], qx_ref[pl.ds(r0, C), ks].astype(F32),
                kx_ref[pl.ds(r0, C), ks].astype(F32), vx_refs[hd][pl.ds(r0, C), :],
                rx_refs[hd][pl.ds(r0, C), :].astype(F32))
        return 0

    lax.fori_loop(0, GLA_RB // C, chunk_body, 0)


def _gla(l, px, smx, pm, smm, wa_pad, b_alpha, gn_g):
    rb = GLA_RB
    nt = SEQ // rb
    row = lambda b, t: b * nt + t
    nbytes = (2 * rb * (2 * GLA_KEY_WIDTH + 2 * GLA_VAL_WIDTH) * 2 + 2 * rb * SMALL_W * 4
              + 2 * rb * GLA_VAL_WIDTH * 2 + GLA_HEADS * GLA_DV * GLA_DK * 4 + 4 * rb * GLA_KEY_WIDTH * 4
              + 2 * SMALL_W * GLA_KEY_WIDTH * 2
              + 2 * MM * (2 * GLA_KEY_WIDTH + 3 * GLA_VAL_WIDTH) * 2 + 4 * MM * GLA_KEY_WIDTH * 4)

    def x_heads(col0):
        return [pl.BlockSpec((rb, GLA_DV), functools.partial(lambda b, t, c: (row(b, t), c), c=col0 // GLA_DV + hd))
                for hd in range(GLA_HEADS)]

    def m_heads(col0):
        return [pl.BlockSpec((MM, GLA_DV), functools.partial(lambda b, t, c: (0, c), c=col0 // GLA_DV + hd))
                for hd in range(GLA_HEADS)]

    return pl.pallas_call(
        _gla_kernel,
        grid=(BATCH, nt),
        in_specs=[
            pl.BlockSpec((rb, GLA_KEY_WIDTH), lambda b, t: (row(b, t), COL_GQ // GLA_KEY_WIDTH)),
            pl.BlockSpec((rb, GLA_KEY_WIDTH), lambda b, t: (row(b, t), COL_GK // GLA_KEY_WIDTH)),
            *x_heads(COL_GV),
            *x_heads(COL_GR),
            pl.BlockSpec((rb, SMALL_W), lambda b, t: (row(b, t), 0)),
            pl.BlockSpec((MM, GLA_KEY_WIDTH), lambda b, t: (0, COL_GQ // GLA_KEY_WIDTH)),
            pl.BlockSpec((MM, GLA_KEY_WIDTH), lambda b, t: (0, COL_GK // GLA_KEY_WIDTH)),
            *m_heads(COL_GV),
            *m_heads(COL_GR),
            pl.BlockSpec((MM, SMALL_W), lambda b, t: (0, 0)),
            _layer_spec(l, (SMALL_W, GLA_KEY_WIDTH), lambda b, t: (0, 0)),
            _layer_spec(l, (1, GLA_KEY_WIDTH), lambda b, t: (0, 0)),
            _layer_spec(l, (1, GLA_VAL_WIDTH), lambda b, t: (0, 0)),
        ],
        out_specs=[
            pl.BlockSpec((rb, GLA_VAL_WIDTH), lambda b, t: (row(b, t), 0)),
            pl.BlockSpec((MM, GLA_VAL_WIDTH), lambda b, t: (0, 0)),
        ],
        out_shape=[jax.ShapeDtypeStruct((MX, GLA_VAL_WIDTH), BF16), jax.ShapeDtypeStruct((MM, GLA_VAL_WIDTH), BF16)],
        scratch_shapes=[
            pltpu.VMEM((GLA_HEADS, GLA_DV, GLA_DK), F32),
            pltpu.VMEM((rb, GLA_KEY_WIDTH), F32),
            pltpu.VMEM((MM, GLA_KEY_WIDTH), F32),
        ],
        compiler_params=_params(("arbitrary", "arbitrary"), nbytes),
        name="gla",
    )(*([px] * (2 + 2 * GLA_HEADS)), smx, *([pm] * (2 + 2 * GLA_HEADS)), smm, wa_pad, b_alpha, gn_g)


def _merge_rows(of_ref, og_ref, wf_ref, wg_ref, gf_ref, gg_ref, y_ref):
    t_fox = _dot(of_ref[...], wf_ref[...])
    t_gla = _dot(og_ref[...], wg_ref[...])
    y = _sigmoid(gf_ref[...].astype(F32)) * t_fox + _sigmoid(gg_ref[...].astype(F32)) * t_gla
    y_ref[...] = y.astype(BF16)


def _merge_kernel(ofx_ref, ogx_ref, gfx_ref, ggx_ref, ofm_ref, ogm_ref, gfm_ref, ggm_ref, wf_ref, wg_ref,
                  yx_ref, ym_ref):
    _merge_rows(ofx_ref, ogx_ref, wf_ref, wg_ref, gfx_ref, ggx_ref, yx_ref)

    @pl.when(_is_first_row_block())
    def _():
        _merge_rows(ofm_ref, ogm_ref, wf_ref, wg_ref, gfm_ref, ggm_ref, ym_ref)


def _merge(l, ofx, ogx, px, ofm, ogm, pm, w_o_fox, w_o_gla):
    bm, bn = MERGE_BM, MERGE_BN
    nj = D_MODEL // bn
    nbytes = (2 * (bm + MM) * (FOX_WIDTH + GLA_VAL_WIDTH) * 2 + 2 * (FOX_WIDTH + GLA_VAL_WIDTH) * bn * 2
              + 6 * (bm + MM) * bn * 2 + 4 * bm * bn * 4)
    gate = lambda col0: (lambda i, j: (i, col0 // bn + j))
    mgate = lambda col0: (lambda i, j: (0, col0 // bn + jnp.where(i == 0, j, nj - 1)))
    return pl.pallas_call(
        _merge_kernel,
        grid=(MX // bm, nj),
        in_specs=[
            pl.BlockSpec((bm, FOX_WIDTH), lambda i, j: (i, 0)),
            pl.BlockSpec((bm, GLA_VAL_WIDTH), lambda i, j: (i, 0)),
            pl.BlockSpec((bm, bn), gate(COL_GATE_FOX)),
            pl.BlockSpec((bm, bn), gate(COL_GATE_GLA)),
            pl.BlockSpec((None, MM, FOX_WIDTH), lambda i, j: (0, 0, 0)),
            pl.BlockSpec((MM, GLA_VAL_WIDTH), lambda i, j: (0, 0)),
            pl.BlockSpec((MM, bn), mgate(COL_GATE_FOX)),
            pl.BlockSpec((MM, bn), mgate(COL_GATE_GLA)),
            _layer_spec(l, (FOX_WIDTH, bn), lambda i, j: (0, j)),
            _layer_spec(l, (GLA_VAL_WIDTH, bn), lambda i, j: (0, j)),
        ],
        out_specs=[pl.BlockSpec((bm, bn), lambda i, j: (i, j)), pl.BlockSpec((MM, bn), _meta_col(nj))],
        out_shape=[jax.ShapeDtypeStruct((MX, D_MODEL), BF16), jax.ShapeDtypeStruct((MM, D_MODEL), BF16)],
        compiler_params=_params(("arbitrary", "arbitrary"), nbytes),
        name="merge",
    )(ofx, ogx, px, px, ofm, ogm, pm, pm, w_o_fox, w_o_gla)


def _outproj_kernel(yx_ref, hx_ref, ym_ref, hm_ref, w_ref, ox_ref, om_ref):
    ox_ref[...] = hx_ref[...] + _dot(yx_ref[...], w_ref[...])

    @pl.when(_is_first_row_block())
    def _():
        om_ref[...] = hm_ref[...] + _dot(ym_ref[...], w_ref[...])


def _outproj(l, yx, hx, ym, hm, w_out):
    bm, bn = OUTPROJ_BM, OUTPROJ_BN
    nj = D_MODEL // bn
    nbytes = 2 * (bm + MM) * D_MODEL * 2 + 2 * D_MODEL * bn * 2 + 4 * (bm + MM) * bn * 4 + bm * bn * 4
    return pl.pallas_call(
        _outproj_kernel,
        grid=(MX // bm, nj),
        in_specs=[
            pl.BlockSpec((bm, D_MODEL), lambda i, j: (i, 0)),
            pl.BlockSpec((bm, bn), lambda i, j: (i, j)),
            pl.BlockSpec((MM, D_MODEL), lambda i, j: (0, 0)),
            pl.BlockSpec((MM, bn), _meta_col(nj)),
            _layer_spec(l, (D_MODEL, bn), lambda i, j: (0, j)),
        ],
        out_specs=[pl.BlockSpec((bm, bn), lambda i, j: (i, j)), pl.BlockSpec((MM, bn), _meta_col(nj))],
        out_shape=[jax.ShapeDtypeStruct((MX, D_MODEL), F32), jax.ShapeDtypeStruct((MM, D_MODEL), F32)],
        compiler_params=_params(("arbitrary", "arbitrary"), nbytes),
        name="outproj",
    )(yx, hx, ym, hm, w_out)


def _mlp_step(xn_ref, w1_ref, w2_ref, o_ref):
    u = jnp.maximum(_dot(xn_ref[...], w1_ref[...]), 0.0)
    o_ref[...] += _dot((u * u).astype(BF16), w2_ref[...])


def _mlp_kernel(hx_ref, hm_ref, g_ref, w1_ref, w2_ref, ox_ref, om_ref, xn_ref, xnm_ref):
    j = pl.program_id(1)

    @pl.when(j == 0)
    def _():
        _rmsnorm_rows(hx_ref, g_ref, xn_ref, MLP_BM)
        ox_ref[...] = hx_ref[...]

    @pl.when(jnp.logical_and(_is_first_row_block(), j == 0))
    def _():
        _rmsnorm_rows(hm_ref, g_ref, xnm_ref, MM)
        om_ref[...] = hm_ref[...]

    _mlp_step(xn_ref, w1_ref, w2_ref, ox_ref)

    @pl.when(_is_first_row_block())
    def _():
        _mlp_step(xnm_ref, w1_ref, w2_ref, om_ref)


def _mlp_final_kernel(hx_ref, g_ref, w1_ref, w2_ref, fg_ref, ox_ref, xn_ref):
    j = pl.program_id(1)

    @pl.when(j == 0)
    def _():
        _rmsnorm_rows(hx_ref, g_ref, xn_ref, MLP_BM)
        ox_ref[...] = hx_ref[...]

    _mlp_step(xn_ref, w1_ref, w2_ref, ox_ref)

    @pl.when(j == pl.num_programs(1) - 1)
    def _():
        _rmsnorm_rows(ox_ref, fg_ref, ox_ref, MLP_BM)


def _mlp_bytes():
    bm, bf = MLP_BM, MLP_BF
    return (4 * bm * D_MODEL * 4 + bm * D_MODEL * 2 + 4 * D_MODEL * bf * 2 + bm * bf * 6
            + 4 * MM * D_MODEL * 4 + MM * D_MODEL * 2)


def _mlp(l, hx, hm, g, w1, w2):
    bm, bf = MLP_BM, MLP_BF
    return pl.pallas_call(
        _mlp_kernel,
        grid=(MX // bm, D_FF // bf),
        in_specs=[
            pl.BlockSpec((bm, D_MODEL), lambda i, j: (i, 0)),
            pl.BlockSpec((MM, D_MODEL), lambda i, j: (0, 0)),
            _layer_spec(l, (1, D_MODEL), lambda i, j: (0, 0)),
            pl.BlockSpec((D_MODEL, bf), lambda i, j: (0, j)),
            pl.BlockSpec((bf, D_MODEL), lambda i, j: (j, 0)),
        ],
        out_specs=[pl.BlockSpec((bm, D_MODEL), lambda i, j: (i, 0)), pl.BlockSpec((MM, D_MODEL), lambda i, j: (0, 0))],
        out_shape=[jax.ShapeDtypeStruct((MX, D_MODEL), F32), jax.ShapeDtypeStruct((MM, D_MODEL), F32)],
        scratch_shapes=[pltpu.VMEM((bm, D_MODEL), BF16), pltpu.VMEM((MM, D_MODEL), BF16)],
        compiler_params=_params(("arbitrary", "arbitrary"), _mlp_bytes()),
        name="mlp",
    )(hx, hm, g, w1, w2)


def _mlp_final(l, hx, g, w1, w2, final_g):
    bm, bf = MLP_BM, MLP_BF
    return pl.pallas_call(
        _mlp_final_kernel,
        grid=(MX // bm, D_FF // bf),
        in_specs=[
            pl.BlockSpec((bm, D_MODEL), lambda i, j: (i, 0)),
            _layer_spec(l, (1, D_MODEL), lambda i, j: (0, 0)),
            pl.BlockSpec((D_MODEL, bf), lambda i, j: (0, j)),
            pl.BlockSpec((bf, D_MODEL), lambda i, j: (j, 0)),
            pl.BlockSpec((1, D_MODEL), lambda i, j: (0, 0)),
        ],
        out_specs=pl.BlockSpec((bm, D_MODEL), lambda i, j: (i, 0)),
        out_shape=jax.ShapeDtypeStruct((MX, D_MODEL), F32),
        scratch_shapes=[pltpu.VMEM((bm, D_MODEL), BF16)],
        compiler_params=_params(("arbitrary", "arbitrary"), _mlp_bytes()),
        name="mlp_final",
    )(hx, g, w1, w2, final_g)


WPREP_SEGMENTS = (
    (COL_FQ, 0, FOX_HEAD_DIM ** -0.5 * LOG2E),
    (COL_FK, 0, 1.0),
    (COL_GQ, WIN_GQ0 - COL_GQ, GLA_DK ** -0.5),
    (COL_GK, WIN_GQ0 - COL_GQ, 1.0),
    (COL_GATE_FOX, WIN_GATE0 - COL_GATE_FOX, 1.0),
)


def _tail_slot(row0):
    step = row0 // WPREP_ROWS - 1
    return step, row0 - (step + 1) * WPREP_ROWS


def _wprep_kernel(a_ref, b_ref, o_ref, small_ref):
    i = pl.program_id(1)
    starts = [seg[0] // WPREP_ROWS for seg in WPREP_SEGMENTS] + [N_PROJ // WPREP_ROWS]
    for (_, shift, scale), lo, hi in zip(WPREP_SEGMENTS, starts[:-1], starts[1:]):
        @pl.when(jnp.logical_and(i >= lo, i < hi))
        def _(shift=shift, scale=scale):
            x = a_ref[...] if shift == 0 else jnp.concatenate([a_ref[shift:, :], b_ref[:shift, :]], axis=0)
            o_ref[...] = (x if scale == 1.0 else x * scale).T.astype(BF16)

    @pl.when(i == 0)
    def _():
        small_ref[GLA_RANK + FOX_HEADS:, :] = jnp.zeros((SMALL_W - GLA_RANK - FOX_HEADS, D_MODEL), F32)

    for row0, n, dst in ((WIN_GA0, GLA_RANK, 0), (WIN_FF0, FOX_HEADS, SMALL_FF0)):
        step, off = _tail_slot(row0)

        @pl.when(i == step)
        def _(n=n, dst=dst, off=off):
            small_ref[dst:dst + n, :] = b_ref[off:off + n, :]


def _prep_in_weights(w_in):
    w_t = jnp.swapaxes(w_in, 1, 2)
    rows, tail = WPREP_ROWS, WPREP_TAIL
    for row0, n in ((WIN_GA0, GLA_RANK), (WIN_FF0, FOX_HEADS)):
        assert 0 <= _tail_slot(row0)[1] and _tail_slot(row0)[1] + n <= tail
    nbytes = 2 * (rows + tail) * D_MODEL * 4 + 2 * rows * D_MODEL * 2 + 2 * rows * D_MODEL * 4 + 2 * SMALL_W * D_MODEL * 4
    main, small_t = pl.pallas_call(
        _wprep_kernel,
        grid=(DEPTH, N_PROJ // rows),
        in_specs=[
            pl.BlockSpec((None, rows, D_MODEL), lambda l, i: (l, i, 0)),
            pl.BlockSpec((None, tail, D_MODEL), lambda l, i: (l, (i + 1) * (rows // tail), 0)),
        ],
        out_specs=[
            pl.BlockSpec((None, D_MODEL, rows), lambda l, i: (l, 0, i)),
            pl.BlockSpec((None, SMALL_W, D_MODEL), lambda l, i: (l, 0, 0)),
        ],
        out_shape=[
            jax.ShapeDtypeStruct((DEPTH, D_MODEL, N_PROJ), BF16),
            jax.ShapeDtypeStruct((DEPTH, SMALL_W, D_MODEL), F32),
        ],
        compiler_params=_params(("arbitrary", "arbitrary"), nbytes),
        name="wprep",
    )(w_t, w_t)
    return main, jnp.swapaxes(small_t, 1, 2).astype(BF16)


def kernel(x, meta_tokens, norm_mix_g, w_in, b_forget, w_alpha2, b_alpha, gla_norm_g, w_o_fox, w_o_gla,
           w_out, norm_mlp_g, w_ff1, w_ff2, final_norm_g):
    hx = x.astype(F32).reshape(MX, D_MODEL)
    hm = jnp.pad(meta_tokens.astype(F32), ((0, MM - N_META), (0, 0)))

    w_main, w_small = _prep_in_weights(w_in)
    wa_pad = jnp.pad(w_alpha2, ((0, 0), (0, SMALL_W - GLA_RANK), (0, 0))).astype(BF16)
    w_o_fox_b, w_o_gla_b, w_out_b = w_o_fox.astype(BF16), w_o_gla.astype(BF16), w_out.astype(BF16)
    norm_mix_g3 = norm_mix_g.reshape(DEPTH, 1, D_MODEL)
    norm_mlp_g3 = norm_mlp_g.reshape(DEPTH, 1, D_MODEL)
    b_alpha3 = b_alpha.reshape(DEPTH, 1, GLA_KEY_WIDTH)
    gla_norm_g3 = gla_norm_g.reshape(DEPTH, 1, GLA_VAL_WIDTH)
    final_g = final_norm_g.reshape(1, D_MODEL)

    for l in range(DEPTH):
        px, smx, ffx, pm, smm, ffm, w_ff1_b, w_ff2_b = _inproj(l, hx, hm, norm_mix_g3, w_main, w_small, w_ff1, w_ff2)
        negc = _fox_cumsum(l, b_forget, ffx.reshape(FOX_HEADS, BATCH, SEQ_LANE_ROWS, LANES), ffm)
        ofx, ofm = _fox_attention(px, pm, negc)
        ogx, ogm = _gla(l, px, smx, pm, smm, wa_pad, b_alpha3, gla_norm_g3)
        yx, ym = _merge(l, ofx, ogx, px, ofm, ogm, pm, w_o_fox_b, w_o_gla_b)
        hx, hm = _outproj(l, yx, hx, ym, hm, w_out_b)
        if l < DEPTH - 1:
            hx, hm = _mlp(l, hx, hm, norm_mlp_g3, w_ff1_b, w_ff2_b)
        else:
            hx = _mlp_final(l, hx, norm_mlp_g3, w_ff1_b, w_ff2_b, final_g)

    return hx.reshape(BATCH, SEQ, D_MODEL)
```

```python
import functools

import jax
import jax.numpy as jnp
from jax import lax
from jax.experimental import pallas as pl
from jax.experimental.pallas import tpu as pltpu

F32 = jnp.float32
BF16 = jnp.bfloat16

D_MODEL = 2048
BATCH = 4
SEQ = 4096
DEPTH = 2
N_META = 16
EPS = 1e-6
MASK_VALUE = -1e30
LOG2E = 1.4426950408889634

FOX_HEADS = 8
FOX_HEAD_DIM = 128
FOX_WIDTH = FOX_HEADS * FOX_HEAD_DIM
GLA_HEADS = 4
GLA_DK = 256
GLA_DV = 512
GLA_KEY_WIDTH = GLA_HEADS * GLA_DK
GLA_VAL_WIDTH = GLA_HEADS * GLA_DV
GLA_RANK = 16
GLA_TAU = 16.0
GLA_CHUNK = 128
GLA_PIVOT = 64
D_FF = 4 * D_MODEL

LANES = 128
SUBLANES = 8
V7X_VMEM_LIMIT_CAP = 60 * 1024 * 1024

MX = BATCH * SEQ
MM = LANES
SEQ_LANE_ROWS = SEQ // LANES
NC_META_ROW = SEQ_LANE_ROWS
NC_ROWS = SEQ_LANE_ROWS + SUBLANES

COL_FQ = 0
COL_FK = COL_FQ + FOX_WIDTH
COL_FV = COL_FK + FOX_WIDTH
COL_GQ = COL_FV + FOX_WIDTH
COL_GK = COL_GQ + GLA_KEY_WIDTH
COL_GV = COL_GK + GLA_KEY_WIDTH
COL_GR = COL_GV + GLA_VAL_WIDTH
COL_GATE_FOX = COL_GR + GLA_VAL_WIDTH
COL_GATE_GLA = COL_GATE_FOX + D_MODEL
N_PROJ = COL_GATE_GLA + D_MODEL
WIN_FF0 = 3 * FOX_WIDTH
WIN_GQ0 = WIN_FF0 + FOX_HEADS
WIN_GA0 = WIN_GQ0 + 2 * GLA_KEY_WIDTH + 2 * GLA_VAL_WIDTH
WIN_GATE0 = WIN_GA0 + GLA_RANK
SMALL_W = LANES
SMALL_FF0 = GLA_RANK

NORM_CHUNK = 128
INPROJ_BM, INPROJ_BN = 1024, 1024
MERGE_BM, MERGE_BN = 2048, 512
OUTPROJ_BM, OUTPROJ_BN = 1024, 1024
MLP_BM, MLP_BF = 1024, 1024
FOX_NH = 4
FOX_BQ = 512
FOX_BK = 512
FOX_STRIP = 64
GLA_RB = 512
GLA_N_CASTS = 5
WPREP_ROWS = 512
WPREP_TAIL = 32


def _vmem_limit(nbytes):
    return int(min(V7X_VMEM_LIMIT_CAP, nbytes + (4 << 20)))


def _params(sem, nbytes):
    return pltpu.CompilerParams(dimension_semantics=sem, vmem_limit_bytes=_vmem_limit(nbytes))


def _layer_spec(l, shape, index_map):
    return pl.BlockSpec((None,) + shape, lambda *idx: (l,) + index_map(*idx))


def _meta_col(nj):
    return lambda i, j: (0, jnp.where(i == 0, j, nj - 1))


def _rmsnorm_rows(src_ref, g_ref, dst_ref, rows, chunk=NORM_CHUNK):
    g = g_ref[...]

    def body(c, _):
        r0 = pl.multiple_of(c * chunk, chunk)
        x = src_ref[pl.ds(r0, chunk), :]
        ms = jnp.mean(x * x, axis=-1, keepdims=True)
        dst_ref[pl.ds(r0, chunk), :] = ((x * lax.rsqrt(ms + EPS)) * g).astype(dst_ref.dtype)
        return 0

    lax.fori_loop(0, rows // chunk, body, 0)


def _log_sigmoid(x):
    return jnp.minimum(x, 0.0) - jnp.log(1.0 + jnp.exp(-jnp.abs(x)))


def _sigmoid(x):
    return 1.0 / (1.0 + jnp.exp(-x))


def _dot(a, b):
    return jnp.dot(a, b, preferred_element_type=F32)


def _dot_nt(a, b):
    return lax.dot_general(a, b, (((1,), (1,)), ((), ())), preferred_element_type=F32)


def _dot_tn(a, b):
    return lax.dot_general(a, b, (((0,), (0,)), ((), ())), preferred_element_type=F32)


def _split_bf16(x, n):
    pieces = []
    rem = x
    for _ in range(n):
        hi = rem.astype(BF16)
        pieces.append(hi)
        rem = rem - hi.astype(F32)
    return pieces


def _dot_f32_lhs(x, r, n):
    acc = None
    for p in _split_bf16(x, n):
        t = _dot(p, r)
        acc = t if acc is None else acc + t
    return acc


def _dot_f32_rhs(l, x, n):
    acc = None
    for p in _split_bf16(x, n):
        t = _dot(l, p)
        acc = t if acc is None else acc + t
    return acc


def _is_first_row_block():
    return pl.program_id(0) == 0


def _inproj_kernel(hx_ref, hm_ref, g_ref, w_ref, ws_ref,
                   px_ref, smx_ref, ffx_ref, pm_ref, smm_ref, ffm_ref, xn_ref, xnm_ref):
    j = pl.program_id(1)
    ff_rows = slice(SMALL_FF0, SMALL_FF0 + FOX_HEADS)

    @pl.when(j == 0)
    def _():
        _rmsnorm_rows(hx_ref, g_ref, xn_ref, INPROJ_BM)
        sm = _dot(xn_ref[...], ws_ref[...])
        smx_ref[...] = sm
        ffx_ref[...] = sm.T[ff_rows, :]

    @pl.when(jnp.logical_and(_is_first_row_block(), j == 0))
    def _():
        _rmsnorm_rows(hm_ref, g_ref, xnm_ref, MM)
        sm = _dot(xnm_ref[...], ws_ref[...])
        smm_ref[...] = sm
        ffm_ref[...] = sm.T[ff_rows, :]

    px_ref[...] = _dot(xn_ref[...], w_ref[...]).astype(BF16)

    @pl.when(_is_first_row_block())
    def _():
        pm_ref[...] = _dot(xnm_ref[...], w_ref[...]).astype(BF16)


def _inproj(l, hx, hm, g, w_main, w_small):
    bm, bn = INPROJ_BM, INPROJ_BN
    nj = N_PROJ // bn
    nbytes = (2 * bm * D_MODEL * 4 + 2 * D_MODEL * bn * 2 + 2 * D_MODEL * SMALL_W * 2
              + 2 * bm * bn * 2 + 2 * bm * SMALL_W * 4 + bm * D_MODEL * 2 + bm * bn * 4 + 2 * 8 * bm * 4
              + 2 * MM * D_MODEL * 4 + 2 * MM * bn * 2 + MM * D_MODEL * 2)
    return pl.pallas_call(
        _inproj_kernel,
        grid=(MX // bm, nj),
        in_specs=[
            pl.BlockSpec((bm, D_MODEL), lambda i, j: (i, 0)),
            pl.BlockSpec((MM, D_MODEL), lambda i, j: (0, 0)),
            _layer_spec(l, (1, D_MODEL), lambda i, j: (0, 0)),
            _layer_spec(l, (D_MODEL, bn), lambda i, j: (0, j)),
            _layer_spec(l, (D_MODEL, SMALL_W), lambda i, j: (0, 0)),
        ],
        out_specs=[
            pl.BlockSpec((bm, bn), lambda i, j: (i, j)),
            pl.BlockSpec((bm, SMALL_W), lambda i, j: (i, 0)),
            pl.BlockSpec((FOX_HEADS, bm), lambda i, j: (0, i)),
            pl.BlockSpec((MM, bn), _meta_col(nj)),
            pl.BlockSpec((MM, SMALL_W), lambda i, j: (0, 0)),
            pl.BlockSpec((FOX_HEADS, MM), lambda i, j: (0, 0)),
        ],
        out_shape=[
            jax.ShapeDtypeStruct((MX, N_PROJ), BF16),
            jax.ShapeDtypeStruct((MX, SMALL_W), F32),
            jax.ShapeDtypeStruct((FOX_HEADS, MX), F32),
            jax.ShapeDtypeStruct((MM, N_PROJ), BF16),
            jax.ShapeDtypeStruct((MM, SMALL_W), F32),
            jax.ShapeDtypeStruct((FOX_HEADS, MM), F32),
        ],
        scratch_shapes=[pltpu.VMEM((bm, D_MODEL), BF16), pltpu.VMEM((MM, D_MODEL), BF16)],
        compiler_params=_params(("arbitrary", "arbitrary"), nbytes),
        name="inproj",
    )(hx, hm, g, w_main, w_small)


def _fox_cumsum_kernel(bf_ref, ffx_ref, ffm_ref, nc_ref, *, layer):
    li = lax.broadcasted_iota(jnp.int32, (LANES, LANES), 0)
    lj = lax.broadcasted_iota(jnp.int32, (LANES, LANES), 1)
    upper = jnp.where(li <= lj, 1.0, 0.0).astype(BF16)
    ones = jnp.ones((LANES, LANES), BF16)
    ri = lax.broadcasted_iota(jnp.int32, (SEQ_LANE_ROWS, SEQ_LANE_ROWS), 0)
    rj = lax.broadcasted_iota(jnp.int32, (SEQ_LANE_ROWS, SEQ_LANE_ROWS), 1)
    strict_lower = jnp.where(rj < ri, 1.0, 0.0).astype(BF16)
    head_id = lax.broadcasted_iota(jnp.int32, (FOX_HEADS, LANES), 0)
    lane_id = lax.broadcasted_iota(jnp.int32, (FOX_HEADS, LANES), 1)
    bias = jnp.zeros((FOX_HEADS, LANES), F32)
    for hd in range(FOX_HEADS):
        bias = jnp.where(head_id == hd, bf_ref[layer, hd], bias)
    real = lane_id < N_META
    lf_m = jnp.where(real, _log_sigmoid(ffm_ref[...] + bias), 0.0)
    cum_m = _dot_f32_lhs(lf_m, upper, 3)
    tot_m = _dot_f32_lhs(lf_m, ones, 3)
    nc_meta = jnp.where(real, cum_m * (-LOG2E), MASK_VALUE)
    group_row = lax.broadcasted_iota(jnp.int32, (SUBLANES, LANES), 0)
    for hd in range(FOX_HEADS):
        lf = _log_sigmoid(ffx_ref[hd, 0] + bf_ref[layer, hd])
        c = (_dot_f32_lhs(lf, upper, 3) + _dot_f32_rhs(strict_lower, _dot_f32_lhs(lf, ones, 3), 3)
             + tot_m[hd:hd + 1, :])
        nc_ref[0, hd, :SEQ_LANE_ROWS, :] = c * (-LOG2E)
        nc_ref[0, hd, SEQ_LANE_ROWS:, :] = jnp.where(
            group_row == 0, jnp.broadcast_to(nc_meta[hd:hd + 1, :], (SUBLANES, LANES)), 0.0)


def _fox_cumsum(l, b_forget, ffx_rows, ffm):
    return pl.pallas_call(
        functools.partial(_fox_cumsum_kernel, layer=l),
        grid=(BATCH,),
        in_specs=[
            pl.BlockSpec(memory_space=pltpu.SMEM),
            pl.BlockSpec((FOX_HEADS, 1, SEQ_LANE_ROWS, LANES), lambda b: (0, b, 0, 0)),
            pl.BlockSpec((FOX_HEADS, MM), lambda b: (0, 0)),
        ],
        out_specs=pl.BlockSpec((1, FOX_HEADS, NC_ROWS, LANES), lambda b: (b, 0, 0, 0)),
        out_shape=jax.ShapeDtypeStruct((BATCH, FOX_HEADS, NC_ROWS, LANES), F32),
        compiler_params=_params(("arbitrary",), 8 << 20),
        name="fox_cumsum",
    )(b_forget, ffx_rows, ffm)


def _fox_kernel(qx_ref, kx_ref, vx_ref, qm_ref, km_ref, vm_ref, nc_ref, ox_ref, om_ref,
                vxs_ref, vms_ref, s_ref, p_ref, m_ref, al_ref, acc_ref):
    hd_cols = [slice(hh * FOX_HEAD_DIM, (hh + 1) * FOX_HEAD_DIM) for hh in range(FOX_NH)]
    for hh in range(FOX_NH):
        vxs_ref[hh, :, :FOX_HEAD_DIM] = vx_ref[:, hd_cols[hh]]
        vxs_ref[hh, :, FOX_HEAD_DIM:] = jnp.ones((SEQ, FOX_HEAD_DIM), BF16)
        vms_ref[hh, :, :FOX_HEAD_DIM] = vm_ref[:, hd_cols[hh]]
        vms_ref[hh, :, FOX_HEAD_DIM:] = jnp.ones((MM, FOX_HEAD_DIM), BF16)
    lane_chunks = FOX_BK // LANES
    row_id = lax.broadcasted_iota(jnp.int32, (FOX_BQ, FOX_BK), 0)
    col_id = lax.broadcasted_iota(jnp.int32, (FOX_BQ, FOX_BK), 1)
    causal = col_id <= row_id

    def meta_bias(hh):
        return nc_ref[0, hh, NC_META_ROW:NC_META_ROW + 1, :]

    for hh in range(FOX_NH):
        s = _dot_nt(qm_ref[:, hd_cols[hh]], km_ref[:, hd_cols[hh]]) + meta_bias(hh)
        s = jnp.where(causal[:MM, :MM], s, MASK_VALUE)
        p = jnp.exp2(s - jnp.max(s, axis=1, keepdims=True)).astype(BF16)
        acc = _dot(p, vms_ref[hh])
        om_ref[:, hd_cols[hh]] = (acc[:, :FOX_HEAD_DIM] / acc[:, FOX_HEAD_DIM:]).astype(BF16)

    def q_body(qi, _):
        q0 = pl.multiple_of(qi * FOX_BQ, FOX_BQ)
        for hh in range(FOX_NH):
            s = _dot_nt(qx_ref[pl.ds(q0, FOX_BQ), hd_cols[hh]], km_ref[:, hd_cols[hh]]) + meta_bias(hh)
            m0 = jnp.max(s, axis=1, keepdims=True)
            m_ref[hh] = jnp.broadcast_to(m0, (FOX_BQ, LANES))
            acc_ref[hh] = _dot(jnp.exp2(s - m0).astype(BF16), vms_ref[hh])

        def tile(kj, diagonal):
            k0 = pl.multiple_of(kj * FOX_BK, FOX_BK)
            for hh in range(FOX_NH):
                s = _dot_nt(qx_ref[pl.ds(q0, FOX_BQ), hd_cols[hh]], kx_ref[pl.ds(k0, FOX_BK), hd_cols[hh]])
                s = s + jnp.concatenate(
                    [nc_ref[0, hh, pl.ds(kj * lane_chunks + c, 1), :] for c in range(lane_chunks)], axis=1)
                s_ref[hh] = jnp.where(causal, s, MASK_VALUE) if diagonal else s
            for hh in range(FOX_NH):
                for r in range(0, FOX_BQ, FOX_STRIP):
                    rows = slice(r, r + FOX_STRIP)
                    sv = s_ref[hh, rows, :]
                    m_old = m_ref[hh, rows, :]
                    m_new = jnp.maximum(m_old, jnp.max(sv, axis=1, keepdims=True))
                    m_ref[hh, rows, :] = m_new
                    al_ref[hh, rows, :] = jnp.exp2(m_old - m_new)
                    p_ref[hh, rows, :] = jnp.exp2(sv - jnp.concatenate([m_new] * lane_chunks, axis=1)).astype(BF16)
            for hh in range(FOX_NH):
                alpha = jnp.concatenate([al_ref[hh]] * 2, axis=1)
                acc_ref[hh] = alpha * acc_ref[hh] + _dot(p_ref[hh], vxs_ref[hh, pl.ds(k0, FOX_BK), :])

        def kv_pair(pj, _):
            tile(2 * pj, False)
            tile(2 * pj + 1, False)
            return 0

        lax.fori_loop(0, qi // 2, kv_pair, 0)

        @pl.when(lax.rem(qi, 2) == 1)
        def _():
            tile(qi - 1, False)

        tile(qi, True)
        for hh in range(FOX_NH):
            acc = acc_ref[hh]
            ox_ref[pl.ds(q0, FOX_BQ), hd_cols[hh]] = (
                acc[:, :FOX_HEAD_DIM] / acc[:, FOX_HEAD_DIM:]).astype(BF16)
        return 0

    lax.fori_loop(0, SEQ // FOX_BQ, q_body, 0)


def _fox_attention(px, pm, negc):
    w = FOX_NH * FOX_HEAD_DIM
    xblk, mblk = (SEQ, w), (MM, w)
    nbytes = (2 * 4 * SEQ * w * 2 + 2 * 4 * MM * w * 2 + FOX_NH * (SEQ + MM) * 2 * FOX_HEAD_DIM * 2
              + FOX_NH * FOX_BQ * (FOX_BK * 6 + 2 * LANES * 4 + 2 * FOX_HEAD_DIM * 4)
              + 2 * FOX_NH * NC_ROWS * LANES * 4)
    return pl.pallas_call(
        _fox_kernel,
        grid=(BATCH, FOX_HEADS // FOX_NH),
        in_specs=[
            pl.BlockSpec(xblk, lambda b, g: (b, COL_FQ // w + g)),
            pl.BlockSpec(xblk, lambda b, g: (b, COL_FK // w + g)),
            pl.BlockSpec(xblk, lambda b, g: (b, COL_FV // w + g)),
            pl.BlockSpec(mblk, lambda b, g: (0, COL_FQ // w + g)),
            pl.BlockSpec(mblk, lambda b, g: (0, COL_FK // w + g)),
            pl.BlockSpec(mblk, lambda b, g: (0, COL_FV // w + g)),
            pl.BlockSpec((1, FOX_NH, NC_ROWS, LANES), lambda b, g: (b, g, 0, 0)),
        ],
        out_specs=[pl.BlockSpec(xblk, lambda b, g: (b, g)), pl.BlockSpec((None,) + mblk, lambda b, g: (b, 0, g))],
        out_shape=[jax.ShapeDtypeStruct((MX, FOX_WIDTH), BF16),
                   jax.ShapeDtypeStruct((BATCH, MM, FOX_WIDTH), BF16)],
        scratch_shapes=[
            pltpu.VMEM((FOX_NH, SEQ, 2 * FOX_HEAD_DIM), BF16),
            pltpu.VMEM((FOX_NH, MM, 2 * FOX_HEAD_DIM), BF16),
            pltpu.VMEM((FOX_NH, FOX_BQ, FOX_BK), F32),
            pltpu.VMEM((FOX_NH, FOX_BQ, FOX_BK), BF16),
            pltpu.VMEM((FOX_NH, FOX_BQ, LANES), F32),
            pltpu.VMEM((FOX_NH, FOX_BQ, LANES), F32),
            pltpu.VMEM((FOX_NH, FOX_BQ, 2 * FOX_HEAD_DIM), F32),
        ],
        compiler_params=_params(("arbitrary", "arbitrary"), nbytes),
        name="fox_attn",
    )(px, px, px, pm, pm, pm, negc)


def _gla_kernel(*refs):
    nh = GLA_HEADS
    n_in = 9 + 4 * nh
    for src_ref, dst_ref in zip(refs[n_in:n_in + GLA_N_CASTS], refs[n_in + GLA_N_CASTS + 2:n_in + 2 * GLA_N_CASTS + 2]):
        dst_ref[...] = src_ref[...].astype(BF16)
    qx_ref, kx_ref = refs[0], refs[1]
    vx_refs, rx_refs = refs[2:2 + nh], refs[2 + nh:2 + 2 * nh]
    smx_ref = refs[2 + 2 * nh]
    qm_ref, km_ref = refs[3 + 2 * nh], refs[4 + 2 * nh]
    vm_refs, rm_refs = refs[5 + 2 * nh:5 + 3 * nh], refs[5 + 3 * nh:5 + 4 * nh]
    smm_ref, wa_ref, ba_ref, gn_ref = refs[5 + 4 * nh:n_in]
    ox_ref, om_ref = refs[n_in + GLA_N_CASTS:n_in + GLA_N_CASTS + 2]
    st_ref, b_ref, bm_ref = refs[-3:]
    C = GLA_CHUNK

    ci = lax.broadcasted_iota(jnp.int32, (C, C), 0)
    cj = lax.broadcasted_iota(jnp.int32, (C, C), 1)
    tril = cj <= ci
    tril_ones = jnp.where(tril, 1.0, 0.0).astype(BF16)

    def log2_decay(small):
        z = _dot(small.astype(BF16), wa_ref[...]) + ba_ref[...]
        return _log_sigmoid(z) * (LOG2E / GLA_TAU)

    def chunk(hd, b, qc, kc, vc, gr):
        vs = slice(hd * GLA_DV, (hd + 1) * GLA_DV)
        b_mid = b[GLA_PIVOT - 1:GLA_PIVOT, :]
        b_last = b[C - 1:C, :]
        q_abs = (qc * jnp.exp2(b)).astype(BF16)
        q_dec = (qc * jnp.exp2(b - b_mid)).astype(BF16)
        k_inv = (kc * jnp.exp2(b_mid - b)).astype(BF16)
        k_end = (kc * jnp.exp2(b_last - b)).astype(BF16)
        a = jnp.where(tril, _dot_nt(q_dec, k_inv), 0.0).astype(BF16)
        st = st_ref[hd]
        o = _dot(a, vc) + _dot_nt(q_abs, st.astype(BF16))
        st_ref[hd] = st * jnp.exp2(b_last) + _dot_tn(vc, k_end)
        rms = lax.rsqrt(jnp.mean(o * o, axis=-1, keepdims=True) + EPS)
        return ((o * rms) * gn_ref[:, vs] * (gr * _sigmoid(gr))).astype(BF16)

    @pl.when(pl.program_id(1) == 0)
    def _():
        st_ref[...] = jnp.zeros_like(st_ref)
        real = lax.broadcasted_iota(jnp.int32, (MM, 1), 0) < N_META
        bm_ref[...] = _dot_f32_rhs(tril_ones, jnp.where(real, log2_decay(smm_ref[...]), 0.0), 2)
        for hd in range(nh):
            ks = slice(hd * GLA_DK, (hd + 1) * GLA_DK)
            vs = slice(hd * GLA_DV, (hd + 1) * GLA_DV)
            kc = jnp.where(real, km_ref[:, ks].astype(F32), 0.0)
            om_ref[:, vs] = chunk(hd, bm_ref[:, ks], qm_ref[:, ks].astype(F32), kc, vm_refs[hd][...],
                                  rm_refs[hd][...].astype(F32))

    g = log2_decay(smx_ref[...])
    for c in range(GLA_RB // C):
        b_ref[c * C:(c + 1) * C, :] = _dot_f32_rhs(tril_ones, g[c * C:(c + 1) * C, :], 2)

    def chunk_body(c, _):
        r0 = pl.multiple_of(c * C, C)
        for hd in range(nh):
            ks = slice(hd * GLA_DK, (hd + 1) * GLA_DK)
            vs = slice(hd * GLA_DV, (hd + 1) * GLA_DV)
            ox_ref[pl.ds(r0, C), vs] = chunk(
                hd, b_ref[pl.ds(r0, C), ks], qx_ref[pl.ds(r0, C), ks].astype(F32),
                kx_ref[pl.ds(r0, C), ks].astype(F32), vx_refs[hd][pl.ds(r0, C), :],
                rx_refs[hd][pl.ds(r0, C), :].astype(F32))
        return 0

    lax.fori_loop(0, GLA_RB // C, chunk_body, 0)


def _gla(l, px, smx, pm, smm, wa_pad, b_alpha, gn_g, riders):
    rb = GLA_RB
    nt = SEQ // rb
    row = lambda b, t: b * nt + t
    steps = BATCH * nt
    slab = lambda b, t: (row(b, t), 0)
    slabs = [(w.shape[1] // steps, w.shape[2]) for w in riders]
    nbytes = (2 * rb * (2 * GLA_KEY_WIDTH + 2 * GLA_VAL_WIDTH) * 2 + 2 * rb * SMALL_W * 4
              + 2 * rb * GLA_VAL_WIDTH * 2 + GLA_HEADS * GLA_DV * GLA_DK * 4 + 4 * rb * GLA_KEY_WIDTH * 4
              + 2 * SMALL_W * GLA_KEY_WIDTH * 2
              + 2 * MM * (2 * GLA_KEY_WIDTH + 3 * GLA_VAL_WIDTH) * 2 + 4 * MM * GLA_KEY_WIDTH * 4)

    def x_heads(col0):
        return [pl.BlockSpec((rb, GLA_DV), functools.partial(lambda b, t, c: (row(b, t), c), c=col0 // GLA_DV + hd))
                for hd in range(GLA_HEADS)]

    def m_heads(col0):
        return [pl.BlockSpec((MM, GLA_DV), functools.partial(lambda b, t, c: (0, c), c=col0 // GLA_DV + hd))
                for hd in range(GLA_HEADS)]

    return pl.pallas_call(
        _gla_kernel,
        grid=(BATCH, nt),
        in_specs=[
            pl.BlockSpec((rb, GLA_KEY_WIDTH), lambda b, t: (row(b, t), COL_GQ // GLA_KEY_WIDTH)),
            pl.BlockSpec((rb, GLA_KEY_WIDTH), lambda b, t: (row(b, t), COL_GK // GLA_KEY_WIDTH)),
            *x_heads(COL_GV),
            *x_heads(COL_GR),
            pl.BlockSpec((rb, SMALL_W), lambda b, t: (row(b, t), 0)),
            pl.BlockSpec((MM, GLA_KEY_WIDTH), lambda b, t: (0, COL_GQ // GLA_KEY_WIDTH)),
            pl.BlockSpec((MM, GLA_KEY_WIDTH), lambda b, t: (0, COL_GK // GLA_KEY_WIDTH)),
            *m_heads(COL_GV),
            *m_heads(COL_GR),
            pl.BlockSpec((MM, SMALL_W), lambda b, t: (0, 0)),
            _layer_spec(l, (SMALL_W, GLA_KEY_WIDTH), lambda b, t: (0, 0)),
            _layer_spec(l, (1, GLA_KEY_WIDTH), lambda b, t: (0, 0)),
            _layer_spec(l, (1, GLA_VAL_WIDTH), lambda b, t: (0, 0)),
            *[_layer_spec(l, sh, slab) for sh in slabs],
        ],
        out_specs=[
            pl.BlockSpec((rb, GLA_VAL_WIDTH), lambda b, t: (row(b, t), 0)),
            pl.BlockSpec((MM, GLA_VAL_WIDTH), lambda b, t: (0, 0)),
            *[pl.BlockSpec(sh, slab) for sh in slabs],
        ],
        out_shape=[jax.ShapeDtypeStruct((MX, GLA_VAL_WIDTH), BF16), jax.ShapeDtypeStruct((MM, GLA_VAL_WIDTH), BF16),
                   *[jax.ShapeDtypeStruct(w.shape[1:], BF16) for w in riders]],
        scratch_shapes=[
            pltpu.VMEM((GLA_HEADS, GLA_DV, GLA_DK), F32),
            pltpu.VMEM((rb, GLA_KEY_WIDTH), F32),
            pltpu.VMEM((MM, GLA_KEY_WIDTH), F32),
        ],
        compiler_params=_params(("arbitrary", "arbitrary"), nbytes + sum(12 * r * c for r, c in slabs)),
        name="gla",
    )(*([px] * (2 + 2 * GLA_HEADS)), smx, *([pm] * (2 + 2 * GLA_HEADS)), smm, wa_pad, b_alpha, gn_g, *riders)


def _merge_rows(of_ref, og_ref, wf_ref, wg_ref, gf_ref, gg_ref, y_ref):
    t_fox = _dot(of_ref[...], wf_ref[...])
    t_gla = _dot(og_ref[...], wg_ref[...])
    y = _sigmoid(gf_ref[...].astype(F32)) * t_fox + _sigmoid(gg_ref[...].astype(F32)) * t_gla
    y_ref[...] = y.astype(BF16)


def _merge_kernel(ofx_ref, ogx_ref, gfx_ref, ggx_ref, ofm_ref, ogm_ref, gfm_ref, ggm_ref, wf_ref, wg_ref,
                  yx_ref, ym_ref):
    _merge_rows(ofx_ref, ogx_ref, wf_ref, wg_ref, gfx_ref, ggx_ref, yx_ref)

    @pl.when(_is_first_row_block())
    def _():
        _merge_rows(ofm_ref, ogm_ref, wf_ref, wg_ref, gfm_ref, ggm_ref, ym_ref)


def _merge(l, ofx, ogx, px, ofm, ogm, pm, w_o_fox, w_o_gla):
    bm, bn = MERGE_BM, MERGE_BN
    nj = D_MODEL // bn
    nbytes = (2 * (bm + MM) * (FOX_WIDTH + GLA_VAL_WIDTH) * 2 + 2 * (FOX_WIDTH + GLA_VAL_WIDTH) * bn * 2
              + 6 * (bm + MM) * bn * 2 + 4 * bm * bn * 4)
    gate = lambda col0: (lambda i, j: (i, col0 // bn + j))
    mgate = lambda col0: (lambda i, j: (0, col0 // bn + jnp.where(i == 0, j, nj - 1)))
    return pl.pallas_call(
        _merge_kernel,
        grid=(MX // bm, nj),
        in_specs=[
            pl.BlockSpec((bm, FOX_WIDTH), lambda i, j: (i, 0)),
            pl.BlockSpec((bm, GLA_VAL_WIDTH), lambda i, j: (i, 0)),
            pl.BlockSpec((bm, bn), gate(COL_GATE_FOX)),
            pl.BlockSpec((bm, bn), gate(COL_GATE_GLA)),
            pl.BlockSpec((None, MM, FOX_WIDTH), lambda i, j: (0, 0, 0)),
            pl.BlockSpec((MM, GLA_VAL_WIDTH), lambda i, j: (0, 0)),
            pl.BlockSpec((MM, bn), mgate(COL_GATE_FOX)),
            pl.BlockSpec((MM, bn), mgate(COL_GATE_GLA)),
            pl.BlockSpec((FOX_WIDTH, bn), lambda i, j: (0, j)),
            pl.BlockSpec((GLA_VAL_WIDTH, bn), lambda i, j: (0, j)),
        ],
        out_specs=[pl.BlockSpec((bm, bn), lambda i, j: (i, j)), pl.BlockSpec((MM, bn), _meta_col(nj))],
        out_shape=[jax.ShapeDtypeStruct((MX, D_MODEL), BF16), jax.ShapeDtypeStruct((MM, D_MODEL), BF16)],
        compiler_params=_params(("arbitrary", "arbitrary"), nbytes),
        name="merge",
    )(ofx, ogx, px, px, ofm, ogm, pm, pm, w_o_fox, w_o_gla)


def _outproj_kernel(yx_ref, hx_ref, ym_ref, hm_ref, w_ref, ox_ref, om_ref):
    ox_ref[...] = hx_ref[...] + _dot(yx_ref[...], w_ref[...])

    @pl.when(_is_first_row_block())
    def _():
        om_ref[...] = hm_ref[...] + _dot(ym_ref[...], w_ref[...])


def _outproj(l, yx, hx, ym, hm, w_out):
    bm, bn = OUTPROJ_BM, OUTPROJ_BN
    nj = D_MODEL // bn
    nbytes = 2 * (bm + MM) * D_MODEL * 2 + 2 * D_MODEL * bn * 2 + 4 * (bm + MM) * bn * 4 + bm * bn * 4
    return pl.pallas_call(
        _outproj_kernel,
        grid=(MX // bm, nj),
        in_specs=[
            pl.BlockSpec((bm, D_MODEL), lambda i, j: (i, 0)),
            pl.BlockSpec((bm, bn), lambda i, j: (i, j)),
            pl.BlockSpec((MM, D_MODEL), lambda i, j: (0, 0)),
            pl.BlockSpec((MM, bn), _meta_col(nj)),
            pl.BlockSpec((D_MODEL, bn), lambda i, j: (0, j)),
        ],
        out_specs=[pl.BlockSpec((bm, bn), lambda i, j: (i, j)), pl.BlockSpec((MM, bn), _meta_col(nj))],
        out_shape=[jax.ShapeDtypeStruct((MX, D_MODEL), F32), jax.ShapeDtypeStruct((MM, D_MODEL), F32)],
        compiler_params=_params(("arbitrary", "arbitrary"), nbytes),
        name="outproj",
    )(yx, hx, ym, hm, w_out)


def _mlp_step(xn_ref, w1_ref, w2_ref, o_ref):
    u = jnp.maximum(_dot(xn_ref[...], w1_ref[...]), 0.0)
    o_ref[...] += _dot((u * u).astype(BF16), w2_ref[...])


def _mlp_kernel(hx_ref, hm_ref, g_ref, w1_ref, w2_ref, ox_ref, om_ref, xn_ref, xnm_ref):
    j = pl.program_id(1)

    @pl.when(j == 0)
    def _():
        _rmsnorm_rows(hx_ref, g_ref, xn_ref, MLP_BM)
        ox_ref[...] = hx_ref[...]

    @pl.when(jnp.logical_and(_is_first_row_block(), j == 0))
    def _():
        _rmsnorm_rows(hm_ref, g_ref, xnm_ref, MM)
        om_ref[...] = hm_ref[...]

    _mlp_step(xn_ref, w1_ref, w2_ref, ox_ref)

    @pl.when(_is_first_row_block())
    def _():
        _mlp_step(xnm_ref, w1_ref, w2_ref, om_ref)


def _mlp_final_kernel(hx_ref, g_ref, w1_ref, w2_ref, fg_ref, ox_ref, xn_ref):
    j = pl.program_id(1)

    @pl.when(j == 0)
    def _():
        _rmsnorm_rows(hx_ref, g_ref, xn_ref, MLP_BM)
        ox_ref[...] = hx_ref[...]

    _mlp_step(xn_ref, w1_ref, w2_ref, ox_ref)

    @pl.when(j == pl.num_programs(1) - 1)
    def _():
        _rmsnorm_rows(ox_ref, fg_ref, ox_ref, MLP_BM)


def _mlp_bytes():
    bm, bf = MLP_BM, MLP_BF
    return (3 * bm * D_MODEL * 4 + bm * D_MODEL * 2 + 4 * D_MODEL * bf * 2 + bm * bf * 2
            + 4 * MM * D_MODEL * 4 + MM * D_MODEL * 2)


def _mlp_rows_spec():
    return pl.BlockSpec((MLP_BM, D_MODEL), lambda i, j: (i, 0), pipeline_mode=pl.Buffered(1))


def _mlp(l, hx, hm, g, w1, w2):
    bm, bf = MLP_BM, MLP_BF
    return pl.pallas_call(
        _mlp_kernel,
        grid=(MX // bm, D_FF // bf),
        in_specs=[
            _mlp_rows_spec(),
            pl.BlockSpec((MM, D_MODEL), lambda i, j: (0, 0)),
            _layer_spec(l, (1, D_MODEL), lambda i, j: (0, 0)),
            pl.BlockSpec((D_MODEL, bf), lambda i, j: (0, j)),
            pl.BlockSpec((bf, D_MODEL), lambda i, j: (j, 0)),
        ],
        out_specs=[pl.BlockSpec((bm, D_MODEL), lambda i, j: (i, 0)), pl.BlockSpec((MM, D_MODEL), lambda i, j: (0, 0))],
        out_shape=[jax.ShapeDtypeStruct((MX, D_MODEL), F32), jax.ShapeDtypeStruct((MM, D_MODEL), F32)],
        scratch_shapes=[pltpu.VMEM((bm, D_MODEL), BF16), pltpu.VMEM((MM, D_MODEL), BF16)],
        compiler_params=_params(("arbitrary", "arbitrary"), _mlp_bytes()),
        name="mlp",
    )(hx, hm, g, w1, w2)


def _mlp_final(l, hx, g, w1, w2, final_g):
    bm, bf = MLP_BM, MLP_BF
    return pl.pallas_call(
        _mlp_final_kernel,
        grid=(MX // bm, D_FF // bf),
        in_specs=[
            _mlp_rows_spec(),
            _layer_spec(l, (1, D_MODEL), lambda i, j: (0, 0)),
            pl.BlockSpec((D_MODEL, bf), lambda i, j: (0, j)),
            pl.BlockSpec((bf, D_MODEL), lambda i, j: (j, 0)),
            pl.BlockSpec((1, D_MODEL), lambda i, j: (0, 0)),
        ],
        out_specs=pl.BlockSpec((bm, D_MODEL), lambda i, j: (i, 0)),
        out_shape=jax.ShapeDtypeStruct((MX, D_MODEL), F32),
        scratch_shapes=[pltpu.VMEM((bm, D_MODEL), BF16)],
        compiler_params=_params(("arbitrary", "arbitrary"), _mlp_bytes()),
        name="mlp_final",
    )(hx, g, w1, w2, final_g)


WPREP_SEGMENTS = (
    (COL_FQ, 0, FOX_HEAD_DIM ** -0.5 * LOG2E),
    (COL_FK, 0, 1.0),
    (COL_GQ, WIN_GQ0 - COL_GQ, GLA_DK ** -0.5),
    (COL_GK, WIN_GQ0 - COL_GQ, 1.0),
    (COL_GATE_FOX, WIN_GATE0 - COL_GATE_FOX, 1.0),
)


def _tail_slot(row0):
    step = row0 // WPREP_ROWS - 1
    return step, row0 - (step + 1) * WPREP_ROWS


def _wprep_kernel(a_ref, b_ref, o_ref, small_ref):
    i = pl.program_id(1)
    starts = [seg[0] // WPREP_ROWS for seg in WPREP_SEGMENTS] + [N_PROJ // WPREP_ROWS]
    for (_, shift, scale), lo, hi in zip(WPREP_SEGMENTS, starts[:-1], starts[1:]):
        @pl.when(jnp.logical_and(i >= lo, i < hi))
        def _(shift=shift, scale=scale):
            x = a_ref[...] if shift == 0 else jnp.concatenate([a_ref[shift:, :], b_ref[:shift, :]], axis=0)
            o_ref[...] = (x if scale == 1.0 else x * scale).T.astype(BF16)

    @pl.when(i == 0)
    def _():
        small_ref[GLA_RANK + FOX_HEADS:, :] = jnp.zeros((SMALL_W - GLA_RANK - FOX_HEADS, D_MODEL), F32)

    for row0, n, dst in ((WIN_GA0, GLA_RANK, 0), (WIN_FF0, FOX_HEADS, SMALL_FF0)):
        step, off = _tail_slot(row0)

        @pl.when(i == step)
        def _(n=n, dst=dst, off=off):
            small_ref[dst:dst + n, :] = b_ref[off:off + n, :]


def _prep_in_weights(w_in):
    w_t = jnp.swapaxes(w_in, 1, 2)
    rows, tail = WPREP_ROWS, WPREP_TAIL
    for row0, n in ((WIN_GA0, GLA_RANK), (WIN_FF0, FOX_HEADS)):
        assert 0 <= _tail_slot(row0)[1] and _tail_slot(row0)[1] + n <= tail
    nbytes = 2 * (rows + tail) * D_MODEL * 4 + 2 * rows * D_MODEL * 2 + 2 * rows * D_MODEL * 4 + 2 * SMALL_W * D_MODEL * 4
    main, small_t = pl.pallas_call(
        _wprep_kernel,
        grid=(DEPTH, N_PROJ // rows),
        in_specs=[
            pl.BlockSpec((None, rows, D_MODEL), lambda l, i: (l, i, 0)),
            pl.BlockSpec((None, tail, D_MODEL), lambda l, i: (l, (i + 1) * (rows // tail), 0)),
        ],
        out_specs=[
            pl.BlockSpec((None, D_MODEL, rows), lambda l, i: (l, 0, i)),
            pl.BlockSpec((None, SMALL_W, D_MODEL), lambda l, i: (l, 0, 0)),
        ],
        out_shape=[
            jax.ShapeDtypeStruct((DEPTH, D_MODEL, N_PROJ), BF16),
            jax.ShapeDtypeStruct((DEPTH, SMALL_W, D_MODEL), F32),
        ],
        compiler_params=_params(("arbitrary", "arbitrary"), nbytes),
        name="wprep",
    )(w_t, w_t)
    return main, jnp.swapaxes(small_t, 1, 2).astype(BF16)


def kernel(x, meta_tokens, norm_mix_g, w_in, b_forget, w_alpha2, b_alpha, gla_norm_g, w_o_fox, w_o_gla,
           w_out, norm_mlp_g, w_ff1, w_ff2, final_norm_g):
    hx = x.astype(F32).reshape(MX, D_MODEL)
    hm = jnp.pad(meta_tokens.astype(F32), ((0, MM - N_META), (0, 0)))

    w_main, w_small = _prep_in_weights(w_in)
    wa_pad = jnp.pad(w_alpha2, ((0, 0), (0, SMALL_W - GLA_RANK), (0, 0))).astype(BF16)
    norm_mix_g3 = norm_mix_g.reshape(DEPTH, 1, D_MODEL)
    norm_mlp_g3 = norm_mlp_g.reshape(DEPTH, 1, D_MODEL)
    b_alpha3 = b_alpha.reshape(DEPTH, 1, GLA_KEY_WIDTH)
    gla_norm_g3 = gla_norm_g.reshape(DEPTH, 1, GLA_VAL_WIDTH)
    final_g = final_norm_g.reshape(1, D_MODEL)

    for l in range(DEPTH):
        px, smx, ffx, pm, smm, ffm = _inproj(l, hx, hm, norm_mix_g3, w_main, w_small)
        negc = _fox_cumsum(l, b_forget, ffx.reshape(FOX_HEADS, BATCH, SEQ_LANE_ROWS, LANES), ffm)
        ofx, ofm = _fox_attention(px, pm, negc)
        ogx, ogm, w_o_fox_b, w_o_gla_b, w_out_b, w_ff1_b, w_ff2_b = _gla(
            l, px, smx, pm, smm, wa_pad, b_alpha3, gla_norm_g3, (w_o_fox, w_o_gla, w_out, w_ff1, w_ff2))
        yx, ym = _merge(l, ofx, ogx, px, ofm, ogm, pm, w_o_fox_b, w_o_gla_b)
        hx, hm = _outproj(l, yx, hx, ym, hm, w_out_b)
        if l < DEPTH - 1:
            hx, hm = _mlp(l, hx, hm, norm_mlp_g3, w_ff1_b, w_ff2_b)
        else:
            hx = _mlp_final(l, hx, norm_mlp_g3, w_ff1_b, w_ff2_b, final_g)

    return hx.reshape(BATCH, SEQ, D_MODEL)
```

```python
import functools

import jax
import jax.numpy as jnp
from jax import lax
from jax.experimental import pallas as pl
from jax.experimental.pallas import tpu as pltpu

F32 = jnp.float32
BF16 = jnp.bfloat16

D_MODEL = 2048
BATCH = 4
SEQ = 4096
DEPTH = 2
N_META = 16
EPS = 1e-6
MASK_VALUE = -1e30
LOG2E = 1.4426950408889634

FOX_HEADS = 8
FOX_HEAD_DIM = 128
FOX_WIDTH = FOX_HEADS * FOX_HEAD_DIM
GLA_HEADS = 4
GLA_DK = 256
GLA_DV = 512
GLA_KEY_WIDTH = GLA_HEADS * GLA_DK
GLA_VAL_WIDTH = GLA_HEADS * GLA_DV
GLA_RANK = 16
GLA_TAU = 16.0
GLA_CHUNK = 128
GLA_PIVOT = 64
D_FF = 4 * D_MODEL

LANES = 128
SUBLANES = 8
V7X_VMEM_LIMIT_CAP = 60 * 1024 * 1024

MX = BATCH * SEQ
MM = LANES
SEQ_LANE_ROWS = SEQ // LANES
NC_META_ROW = SEQ_LANE_ROWS
NC_ROWS = SEQ_LANE_ROWS + SUBLANES

COL_FQ = 0
COL_FK = COL_FQ + FOX_WIDTH
COL_FV = COL_FK + FOX_WIDTH
COL_GQ = COL_FV + FOX_WIDTH
COL_GK = COL_GQ + GLA_KEY_WIDTH
COL_GV = COL_GK + GLA_KEY_WIDTH
COL_GR = COL_GV + GLA_VAL_WIDTH
COL_GATE_FOX = COL_GR + GLA_VAL_WIDTH
COL_GATE_GLA = COL_GATE_FOX + D_MODEL
N_PROJ = COL_GATE_GLA + D_MODEL
WIN_FF0 = 3 * FOX_WIDTH
WIN_GQ0 = WIN_FF0 + FOX_HEADS
WIN_GA0 = WIN_GQ0 + 2 * GLA_KEY_WIDTH + 2 * GLA_VAL_WIDTH
WIN_GATE0 = WIN_GA0 + GLA_RANK
SMALL_W = LANES
SMALL_FF0 = GLA_RANK

NORM_CHUNK = 128
INPROJ_BM, INPROJ_BN = 1024, 1024
MERGE_BM, MERGE_BN = 2048, 512
OUTPROJ_BM, OUTPROJ_BN = 1024, 1024
MLP_BM, MLP_BF = 1024, 512
FOX_NH = 4
FOX_BQ = 512
FOX_BK = 512
FOX_STRIP = 64
GLA_RB = 512
GLA_N_CASTS = 5
WPREP_ROWS = 512
WPREP_TAIL = 32


def _vmem_limit(nbytes):
    return int(min(V7X_VMEM_LIMIT_CAP, nbytes + (4 << 20)))


def _params(sem, nbytes):
    return pltpu.CompilerParams(dimension_semantics=sem, vmem_limit_bytes=_vmem_limit(nbytes))


def _layer_spec(l, shape, index_map):
    return pl.BlockSpec((None,) + shape, lambda *idx: (l,) + index_map(*idx))


def _meta_col(nj):
    return lambda i, j: (0, jnp.where(i == 0, j, nj - 1))


def _rmsnorm_rows(src_ref, g_ref, dst_ref, rows, chunk=NORM_CHUNK):
    g = g_ref[...]

    def body(c, _):
        r0 = pl.multiple_of(c * chunk, chunk)
        x = src_ref[pl.ds(r0, chunk), :]
        ms = jnp.mean(x * x, axis=-1, keepdims=True)
        dst_ref[pl.ds(r0, chunk), :] = ((x * lax.rsqrt(ms + EPS)) * g).astype(dst_ref.dtype)
        return 0

    lax.fori_loop(0, rows // chunk, body, 0)


def _log_sigmoid(x):
    return jnp.minimum(x, 0.0) - jnp.log(1.0 + jnp.exp(-jnp.abs(x)))


def _sigmoid(x):
    return 1.0 / (1.0 + jnp.exp(-x))


def _dot(a, b):
    return jnp.dot(a, b, preferred_element_type=F32)


def _dot_nt(a, b):
    return lax.dot_general(a, b, (((1,), (1,)), ((), ())), preferred_element_type=F32)


def _dot_tn(a, b):
    return lax.dot_general(a, b, (((0,), (0,)), ((), ())), preferred_element_type=F32)


def _split_bf16(x, n):
    pieces = []
    rem = x
    for _ in range(n):
        hi = rem.astype(BF16)
        pieces.append(hi)
        rem = rem - hi.astype(F32)
    return pieces


def _dot_f32_lhs(x, r, n):
    acc = None
    for p in _split_bf16(x, n):
        t = _dot(p, r)
        acc = t if acc is None else acc + t
    return acc


def _dot_f32_rhs(l, x, n):
    acc = None
    for p in _split_bf16(x, n):
        t = _dot(l, p)
        acc = t if acc is None else acc + t
    return acc


def _is_first_row_block():
    return pl.program_id(0) == 0


def _inproj_kernel(hx_ref, hm_ref, g_ref, w_ref, ws_ref,
                   px_ref, smx_ref, ffx_ref, pm_ref, smm_ref, ffm_ref, xn_ref, xnm_ref):
    j = pl.program_id(1)
    ff_rows = slice(SMALL_FF0, SMALL_FF0 + FOX_HEADS)

    @pl.when(j == 0)
    def _():
        _rmsnorm_rows(hx_ref, g_ref, xn_ref, INPROJ_BM)
        sm = _dot(xn_ref[...], ws_ref[...])
        smx_ref[...] = sm
        ffx_ref[...] = sm.T[ff_rows, :]

    @pl.when(jnp.logical_and(_is_first_row_block(), j == 0))
    def _():
        _rmsnorm_rows(hm_ref, g_ref, xnm_ref, MM)
        sm = _dot(xnm_ref[...], ws_ref[...])
        smm_ref[...] = sm
        ffm_ref[...] = sm.T[ff_rows, :]

    px_ref[...] = _dot(xn_ref[...], w_ref[...]).astype(BF16)

    @pl.when(_is_first_row_block())
    def _():
        pm_ref[...] = _dot(xnm_ref[...], w_ref[...]).astype(BF16)


def _inproj(l, hx, hm, g, w_main, w_small):
    bm, bn = INPROJ_BM, INPROJ_BN
    nj = N_PROJ // bn
    nbytes = (2 * bm * D_MODEL * 4 + 2 * D_MODEL * bn * 2 + 2 * D_MODEL * SMALL_W * 2
              + 2 * bm * bn * 2 + 2 * bm * SMALL_W * 4 + bm * D_MODEL * 2 + bm * bn * 4 + 2 * 8 * bm * 4
              + 2 * MM * D_MODEL * 4 + 2 * MM * bn * 2 + MM * D_MODEL * 2)
    return pl.pallas_call(
        _inproj_kernel,
        grid=(MX // bm, nj),
        in_specs=[
            pl.BlockSpec((bm, D_MODEL), lambda i, j: (i, 0)),
            pl.BlockSpec((MM, D_MODEL), lambda i, j: (0, 0)),
            _layer_spec(l, (1, D_MODEL), lambda i, j: (0, 0)),
            _layer_spec(l, (D_MODEL, bn), lambda i, j: (0, j)),
            _layer_spec(l, (D_MODEL, SMALL_W), lambda i, j: (0, 0)),
        ],
        out_specs=[
            pl.BlockSpec((bm, bn), lambda i, j: (i, j)),
            pl.BlockSpec((bm, SMALL_W), lambda i, j: (i, 0)),
            pl.BlockSpec((FOX_HEADS, bm), lambda i, j: (0, i)),
            pl.BlockSpec((MM, bn), _meta_col(nj)),
            pl.BlockSpec((MM, SMALL_W), lambda i, j: (0, 0)),
            pl.BlockSpec((FOX_HEADS, MM), lambda i, j: (0, 0)),
        ],
        out_shape=[
            jax.ShapeDtypeStruct((MX, N_PROJ), BF16),
            jax.ShapeDtypeStruct((MX, SMALL_W), F32),
            jax.ShapeDtypeStruct((FOX_HEADS, MX), F32),
            jax.ShapeDtypeStruct((MM, N_PROJ), BF16),
            jax.ShapeDtypeStruct((MM, SMALL_W), F32),
            jax.ShapeDtypeStruct((FOX_HEADS, MM), F32),
        ],
        scratch_shapes=[pltpu.VMEM((bm, D_MODEL), BF16), pltpu.VMEM((MM, D_MODEL), BF16)],
        compiler_params=_params(("arbitrary", "arbitrary"), nbytes),
        name="inproj",
    )(hx, hm, g, w_main, w_small)


def _fox_cumsum_kernel(bf_ref, ffx_ref, ffm_ref, nc_ref, *, layer):
    li = lax.broadcasted_iota(jnp.int32, (LANES, LANES), 0)
    lj = lax.broadcasted_iota(jnp.int32, (LANES, LANES), 1)
    upper = jnp.where(li <= lj, 1.0, 0.0).astype(BF16)
    ones = jnp.ones((LANES, LANES), BF16)
    ri = lax.broadcasted_iota(jnp.int32, (SEQ_LANE_ROWS, SEQ_LANE_ROWS), 0)
    rj = lax.broadcasted_iota(jnp.int32, (SEQ_LANE_ROWS, SEQ_LANE_ROWS), 1)
    strict_lower = jnp.where(rj < ri, 1.0, 0.0).astype(BF16)
    head_id = lax.broadcasted_iota(jnp.int32, (FOX_HEADS, LANES), 0)
    lane_id = lax.broadcasted_iota(jnp.int32, (FOX_HEADS, LANES), 1)
    bias = jnp.zeros((FOX_HEADS, LANES), F32)
    for hd in range(FOX_HEADS):
        bias = jnp.where(head_id == hd, bf_ref[layer, hd], bias)
    real = lane_id < N_META
    lf_m = jnp.where(real, _log_sigmoid(ffm_ref[...] + bias), 0.0)
    cum_m = _dot_f32_lhs(lf_m, upper, 3)
    tot_m = _dot_f32_lhs(lf_m, ones, 3)
    nc_meta = jnp.where(real, cum_m * (-LOG2E), MASK_VALUE)
    group_row = lax.broadcasted_iota(jnp.int32, (SUBLANES, LANES), 0)
    for hd in range(FOX_HEADS):
        lf = _log_sigmoid(ffx_ref[hd, 0] + bf_ref[layer, hd])
        c = (_dot_f32_lhs(lf, upper, 3) + _dot_f32_rhs(strict_lower, _dot_f32_lhs(lf, ones, 3), 3)
             + tot_m[hd:hd + 1, :])
        nc_ref[0, hd, :SEQ_LANE_ROWS, :] = c * (-LOG2E)
        nc_ref[0, hd, SEQ_LANE_ROWS:, :] = jnp.where(
            group_row == 0, jnp.broadcast_to(nc_meta[hd:hd + 1, :], (SUBLANES, LANES)), 0.0)


def _fox_cumsum(l, b_forget, ffx_rows, ffm):
    return pl.pallas_call(
        functools.partial(_fox_cumsum_kernel, layer=l),
        grid=(BATCH,),
        in_specs=[
            pl.BlockSpec(memory_space=pltpu.SMEM),
            pl.BlockSpec((FOX_HEADS, 1, SEQ_LANE_ROWS, LANES), lambda b: (0, b, 0, 0)),
            pl.BlockSpec((FOX_HEADS, MM), lambda b: (0, 0)),
        ],
        out_specs=pl.BlockSpec((1, FOX_HEADS, NC_ROWS, LANES), lambda b: (b, 0, 0, 0)),
        out_shape=jax.ShapeDtypeStruct((BATCH, FOX_HEADS, NC_ROWS, LANES), F32),
        compiler_params=_params(("arbitrary",), 8 << 20),
        name="fox_cumsum",
    )(b_forget, ffx_rows, ffm)


def _fox_kernel(qx_ref, kx_ref, vx_ref, qm_ref, km_ref, vm_ref, nc_ref, ox_ref, om_ref,
                vxs_ref, vms_ref, s_ref, p_ref, m_ref, al_ref, acc_ref):
    hd_cols = [slice(hh * FOX_HEAD_DIM, (hh + 1) * FOX_HEAD_DIM) for hh in range(FOX_NH)]
    for hh in range(FOX_NH):
        vxs_ref[hh, :, :FOX_HEAD_DIM] = vx_ref[:, hd_cols[hh]]
        vxs_ref[hh, :, FOX_HEAD_DIM:] = jnp.ones((SEQ, FOX_HEAD_DIM), BF16)
        vms_ref[hh, :, :FOX_HEAD_DIM] = vm_ref[:, hd_cols[hh]]
        vms_ref[hh, :, FOX_HEAD_DIM:] = jnp.ones((MM, FOX_HEAD_DIM), BF16)
    lane_chunks = FOX_BK // LANES
    row_id = lax.broadcasted_iota(jnp.int32, (FOX_BQ, FOX_BK), 0)
    col_id = lax.broadcasted_iota(jnp.int32, (FOX_BQ, FOX_BK), 1)
    causal = col_id <= row_id

    def meta_bias(hh):
        return nc_ref[0, hh, NC_META_ROW:NC_META_ROW + 1, :]

    for hh in range(FOX_NH):
        s = _dot_nt(qm_ref[:, hd_cols[hh]], km_ref[:, hd_cols[hh]]) + meta_bias(hh)
        s = jnp.where(causal[:MM, :MM], s, MASK_VALUE)
        p = jnp.exp2(s - jnp.max(s, axis=1, keepdims=True)).astype(BF16)
        acc = _dot(p, vms_ref[hh])
        om_ref[:, hd_cols[hh]] = (acc[:, :FOX_HEAD_DIM] / acc[:, FOX_HEAD_DIM:]).astype(BF16)

    def q_body(qi, _):
        q0 = pl.multiple_of(qi * FOX_BQ, FOX_BQ)
        for hh in range(FOX_NH):
            s = _dot_nt(qx_ref[pl.ds(q0, FOX_BQ), hd_cols[hh]], km_ref[:, hd_cols[hh]]) + meta_bias(hh)
            m0 = jnp.max(s, axis=1, keepdims=True)
            m_ref[hh] = jnp.broadcast_to(m0, (FOX_BQ, LANES))
            acc_ref[hh] = _dot(jnp.exp2(s - m0).astype(BF16), vms_ref[hh])

        def tile(kj, diagonal):
            k0 = pl.multiple_of(kj * FOX_BK, FOX_BK)
            for hh in range(FOX_NH):
                s = _dot_nt(qx_ref[pl.ds(q0, FOX_BQ), hd_cols[hh]], kx_ref[pl.ds(k0, FOX_BK), hd_cols[hh]])
                s = s + jnp.concatenate(
                    [nc_ref[0, hh, pl.ds(kj * lane_chunks + c, 1), :] for c in range(lane_chunks)], axis=1)
                s_ref[hh] = jnp.where(causal, s, MASK_VALUE) if diagonal else s
            for hh in range(FOX_NH):
                for r in range(0, FOX_BQ, FOX_STRIP):
                    rows = slice(r, r + FOX_STRIP)
                    sv = s_ref[hh, rows, :]
                    m_old = m_ref[hh, rows, :]
                    m_new = jnp.maximum(m_old, jnp.max(sv, axis=1, keepdims=True))
                    m_ref[hh, rows, :] = m_new
                    al_ref[hh, rows, :] = jnp.exp2(m_old - m_new)
                    p_ref[hh, rows, :] = jnp.exp2(sv - jnp.concatenate([m_new] * lane_chunks, axis=1)).astype(BF16)
            for hh in range(FOX_NH):
                alpha = jnp.concatenate([al_ref[hh]] * 2, axis=1)
                acc_ref[hh] = alpha * acc_ref[hh] + _dot(p_ref[hh], vxs_ref[hh, pl.ds(k0, FOX_BK), :])

        def kv_pair(pj, _):
            tile(2 * pj, False)
            tile(2 * pj + 1, False)
            return 0

        lax.fori_loop(0, qi // 2, kv_pair, 0)

        @pl.when(lax.rem(qi, 2) == 1)
        def _():
            tile(qi - 1, False)

        tile(qi, True)
        for hh in range(FOX_NH):
            acc = acc_ref[hh]
            ox_ref[pl.ds(q0, FOX_BQ), hd_cols[hh]] = (
                acc[:, :FOX_HEAD_DIM] / acc[:, FOX_HEAD_DIM:]).astype(BF16)
        return 0

    lax.fori_loop(0, SEQ // FOX_BQ, q_body, 0)


def _fox_attention(px, pm, negc):
    w = FOX_NH * FOX_HEAD_DIM
    xblk, mblk = (SEQ, w), (MM, w)
    nbytes = (2 * 4 * SEQ * w * 2 + 2 * 4 * MM * w * 2 + FOX_NH * (SEQ + MM) * 2 * FOX_HEAD_DIM * 2
              + FOX_NH * FOX_BQ * (FOX_BK * 6 + 2 * LANES * 4 + 2 * FOX_HEAD_DIM * 4)
              + 2 * FOX_NH * NC_ROWS * LANES * 4)
    return pl.pallas_call(
        _fox_kernel,
        grid=(BATCH, FOX_HEADS // FOX_NH),
        in_specs=[
            pl.BlockSpec(xblk, lambda b, g: (b, COL_FQ // w + g)),
            pl.BlockSpec(xblk, lambda b, g: (b, COL_FK // w + g)),
            pl.BlockSpec(xblk, lambda b, g: (b, COL_FV // w + g)),
            pl.BlockSpec(mblk, lambda b, g: (0, COL_FQ // w + g)),
            pl.BlockSpec(mblk, lambda b, g: (0, COL_FK // w + g)),
            pl.BlockSpec(mblk, lambda b, g: (0, COL_FV // w + g)),
            pl.BlockSpec((1, FOX_NH, NC_ROWS, LANES), lambda b, g: (b, g, 0, 0)),
        ],
        out_specs=[pl.BlockSpec(xblk, lambda b, g: (b, g)), pl.BlockSpec((None,) + mblk, lambda b, g: (b, 0, g))],
        out_shape=[jax.ShapeDtypeStruct((MX, FOX_WIDTH), BF16),
                   jax.ShapeDtypeStruct((BATCH, MM, FOX_WIDTH), BF16)],
        scratch_shapes=[
            pltpu.VMEM((FOX_NH, SEQ, 2 * FOX_HEAD_DIM), BF16),
            pltpu.VMEM((FOX_NH, MM, 2 * FOX_HEAD_DIM), BF16),
            pltpu.VMEM((FOX_NH, FOX_BQ, FOX_BK), F32),
            pltpu.VMEM((FOX_NH, FOX_BQ, FOX_BK), BF16),
            pltpu.VMEM((FOX_NH, FOX_BQ, LANES), F32),
            pltpu.VMEM((FOX_NH, FOX_BQ, LANES), F32),
            pltpu.VMEM((FOX_NH, FOX_BQ, 2 * FOX_HEAD_DIM), F32),
        ],
        compiler_params=_params(("arbitrary", "arbitrary"), nbytes),
        name="fox_attn",
    )(px, px, px, pm, pm, pm, negc)


def _gla_kernel(*refs):
    nh = GLA_HEADS
    n_in = 9 + 4 * nh
    for src_ref, dst_ref in zip(refs[n_in:n_in + GLA_N_CASTS], refs[n_in + GLA_N_CASTS + 2:n_in + 2 * GLA_N_CASTS + 2]):
        dst_ref[...] = src_ref[...].astype(BF16)
    qx_ref, kx_ref = refs[0], refs[1]
    vx_refs, rx_refs = refs[2:2 + nh], refs[2 + nh:2 + 2 * nh]
    smx_ref = refs[2 + 2 * nh]
    qm_ref, km_ref = refs[3 + 2 * nh], refs[4 + 2 * nh]
    vm_refs, rm_refs = refs[5 + 2 * nh:5 + 3 * nh], refs[5 + 3 * nh:5 + 4 * nh]
    smm_ref, wa_ref, ba_ref, gn_ref = refs[5 + 4 * nh:n_in]
    ox_ref, om_ref = refs[n_in + GLA_N_CASTS:n_in + GLA_N_CASTS + 2]
    st_ref, b_ref, bm_ref = refs[-3:]
    C = GLA_CHUNK

    ci = lax.broadcasted_iota(jnp.int32, (C, C), 0)
    cj = lax.broadcasted_iota(jnp.int32, (C, C), 1)
    tril = cj <= ci
    tril_ones = jnp.where(tril, 1.0, 0.0).astype(BF16)

    def log2_decay(small):
        z = _dot(small.astype(BF16), wa_ref[...]) + ba_ref[...]
        return _log_sigmoid(z) * (LOG2E / GLA_TAU)

    def chunk(hd, b, qc, kc, vc, gr):
        vs = slice(hd * GLA_DV, (hd + 1) * GLA_DV)
        b_mid = b[GLA_PIVOT - 1:GLA_PIVOT, :]
        b_last = b[C - 1:C, :]
        q_abs = (qc * jnp.exp2(b)).astype(BF16)
        q_dec = (qc * jnp.exp2(b - b_mid)).astype(BF16)
        k_inv = (kc * jnp.exp2(b_mid - b)).astype(BF16)
        k_end = (kc * jnp.exp2(b_last - b)).astype(BF16)
        a = jnp.where(tril, _dot_nt(q_dec, k_inv), 0.0).astype(BF16)
        st = st_ref[hd]
        o = _dot(a, vc) + _dot_nt(q_abs, st.astype(BF16))
        st_ref[hd] = st * jnp.exp2(b_last) + _dot_tn(vc, k_end)
        rms = lax.rsqrt(jnp.mean(o * o, axis=-1, keepdims=True) + EPS)
        return ((o * rms) * gn_ref[:, vs] * (gr * _sigmoid(gr))).astype(BF16)

    @pl.when(pl.program_id(1) == 0)
    def _():
        st_ref[...] = jnp.zeros_like(st_ref)
        real = lax.broadcasted_iota(jnp.int32, (MM, 1), 0) < N_META
        bm_ref[...] = _dot_f32_rhs(tril_ones, jnp.where(real, log2_decay(smm_ref[...]), 0.0), 2)
        for hd in range(nh):
            ks = slice(hd * GLA_DK, (hd + 1) * GLA_DK)
            vs = slice(hd * GLA_DV, (hd + 1) * GLA_DV)
            kc = jnp.where(real, km_ref[:, ks].astype(F32), 0.0)
            om_ref[:, vs] = chunk(hd, bm_ref[:, ks], qm_ref[:, ks].astype(F32), kc, vm_refs[hd][...],
                                  rm_refs[hd][...].astype(F32))

    g = log2_decay(smx_ref[...])
    for c in range(GLA_RB // C):
        b_ref[c * C:(c + 1) * C, :] = _dot_f32_rhs(tril_ones, g[c * C:(c + 1) * C, :], 2)

    def chunk_body(c, _):
        r0 = pl.multiple_of(c * C, C)
        for hd in range(nh):
            ks = slice(hd * GLA_DK, (hd + 1) * GLA_DK)
            vs = slice(hd * GLA_DV, (hd + 1) * GLA_DV)
            ox_ref[pl.ds(r0, C), vs] = chunk(
                hd, b_ref[pl.ds(r0, C), ks], qx_ref[pl.ds(r0, C), ks].astype(F32),
                kx_ref[pl.ds(r0, C), ks].astype(F32), vx_refs[hd][pl.ds(r0, C), :],
                rx_refs[hd][pl.ds(r0, C), :].astype(F32))
        return 0

    lax.fori_loop(0, GLA_RB // C, chunk_body, 0)


def _gla(l, px, smx, pm, smm, wa_pad, b_alpha, gn_g, riders):
    rb = GLA_RB
    nt = SEQ // rb
    row = lambda b, t: b * nt + t
    steps = BATCH * nt
    slab = lambda b, t: (row(b, t), 0)
    slabs = [(w.shape[1] // steps, w.shape[2]) for w in riders]
    nbytes = (2 * rb * (2 * GLA_KEY_WIDTH + 2 * GLA_VAL_WIDTH) * 2 + 2 * rb * SMALL_W * 4
              + 2 * rb * GLA_VAL_WIDTH * 2 + GLA_HEADS * GLA_DV * GLA_DK * 4 + 4 * rb * GLA_KEY_WIDTH * 4
              + 2 * SMALL_W * GLA_KEY_WIDTH * 2
              + 2 * MM * (2 * GLA_KEY_WIDTH + 3 * GLA_VAL_WIDTH) * 2 + 4 * MM * GLA_KEY_WIDTH * 4)

    def x_heads(col0):
        return [pl.BlockSpec((rb, GLA_DV), functools.partial(lambda b, t, c: (row(b, t), c), c=col0 // GLA_DV + hd))
                for hd in range(GLA_HEADS)]

    def m_heads(col0):
        return [pl.BlockSpec((MM, GLA_DV), functools.partial(lambda b, t, c: (0, c), c=col0 // GLA_DV + hd))
                for hd in range(GLA_HEADS)]

    return pl.pallas_call(
        _gla_kernel,
        grid=(BATCH, nt),
        in_specs=[
            pl.BlockSpec((rb, GLA_KEY_WIDTH), lambda b, t: (row(b, t), COL_GQ // GLA_KEY_WIDTH)),
            pl.BlockSpec((rb, GLA_KEY_WIDTH), lambda b, t: (row(b, t), COL_GK // GLA_KEY_WIDTH)),
            *x_heads(COL_GV),
            *x_heads(COL_GR),
            pl.BlockSpec((rb, SMALL_W), lambda b, t: (row(b, t), 0)),
            pl.BlockSpec((MM, GLA_KEY_WIDTH), lambda b, t: (0, COL_GQ // GLA_KEY_WIDTH)),
            pl.BlockSpec((MM, GLA_KEY_WIDTH), lambda b, t: (0, COL_GK // GLA_KEY_WIDTH)),
            *m_heads(COL_GV),
            *m_heads(COL_GR),
            pl.BlockSpec((MM, SMALL_W), lambda b, t: (0, 0)),
            _layer_spec(l, (SMALL_W, GLA_KEY_WIDTH), lambda b, t: (0, 0)),
            _layer_spec(l, (1, GLA_KEY_WIDTH), lambda b, t: (0, 0)),
            _layer_spec(l, (1, GLA_VAL_WIDTH), lambda b, t: (0, 0)),
            *[_layer_spec(l, sh, slab) for sh in slabs],
        ],
        out_specs=[
            pl.BlockSpec((rb, GLA_VAL_WIDTH), lambda b, t: (row(b, t), 0)),
            pl.BlockSpec((MM, GLA_VAL_WIDTH), lambda b, t: (0, 0)),
            *[pl.BlockSpec(sh, slab) for sh in slabs],
        ],
        out_shape=[jax.ShapeDtypeStruct((MX, GLA_VAL_WIDTH), BF16), jax.ShapeDtypeStruct((MM, GLA_VAL_WIDTH), BF16),
                   *[jax.ShapeDtypeStruct(w.shape[1:], BF16) for w in riders]],
        scratch_shapes=[
            pltpu.VMEM((GLA_HEADS, GLA_DV, GLA_DK), F32),
            pltpu.VMEM((rb, GLA_KEY_WIDTH), F32),
            pltpu.VMEM((MM, GLA_KEY_WIDTH), F32),
        ],
        compiler_params=_params(("arbitrary", "arbitrary"), nbytes + sum(12 * r * c for r, c in slabs)),
        name="gla",
    )(*([px] * (2 + 2 * GLA_HEADS)), smx, *([pm] * (2 + 2 * GLA_HEADS)), smm, wa_pad, b_alpha, gn_g, *riders)


def _merge_rows(of_ref, og_ref, wf_ref, wg_ref, gf_ref, gg_ref, y_ref):
    t_fox = _dot(of_ref[...], wf_ref[...])
    t_gla = _dot(og_ref[...], wg_ref[...])
    y = _sigmoid(gf_ref[...].astype(F32)) * t_fox + _sigmoid(gg_ref[...].astype(F32)) * t_gla
    y_ref[...] = y.astype(BF16)


def _merge_kernel(ofx_ref, ogx_ref, gfx_ref, ggx_ref, ofm_ref, ogm_ref, gfm_ref, ggm_ref, wf_ref, wg_ref,
                  yx_ref, ym_ref):
    _merge_rows(ofx_ref, ogx_ref, wf_ref, wg_ref, gfx_ref, ggx_ref, yx_ref)

    @pl.when(_is_first_row_block())
    def _():
        _merge_rows(ofm_ref, ogm_ref, wf_ref, wg_ref, gfm_ref, ggm_ref, ym_ref)


def _merge(l, ofx, ogx, px, ofm, ogm, pm, w_o_fox, w_o_gla):
    bm, bn = MERGE_BM, MERGE_BN
    nj = D_MODEL // bn
    nbytes = (2 * (bm + MM) * (FOX_WIDTH + GLA_VAL_WIDTH) * 2 + 2 * (FOX_WIDTH + GLA_VAL_WIDTH) * bn * 2
              + 6 * (bm + MM) * bn * 2 + 4 * bm * bn * 4)
    gate = lambda col0: (lambda i, j: (i, col0 // bn + j))
    mgate = lambda col0: (lambda i, j: (0, col0 // bn + jnp.where(i == 0, j, nj - 1)))
    return pl.pallas_call(
        _merge_kernel,
        grid=(MX // bm, nj),
        in_specs=[
            pl.BlockSpec((bm, FOX_WIDTH), lambda i, j: (i, 0)),
            pl.BlockSpec((bm, GLA_VAL_WIDTH), lambda i, j: (i, 0)),
            pl.BlockSpec((bm, bn), gate(COL_GATE_FOX)),
            pl.BlockSpec((bm, bn), gate(COL_GATE_GLA)),
            pl.BlockSpec((None, MM, FOX_WIDTH), lambda i, j: (0, 0, 0)),
            pl.BlockSpec((MM, GLA_VAL_WIDTH), lambda i, j: (0, 0)),
            pl.BlockSpec((MM, bn), mgate(COL_GATE_FOX)),
            pl.BlockSpec((MM, bn), mgate(COL_GATE_GLA)),
            pl.BlockSpec((FOX_WIDTH, bn), lambda i, j: (0, j)),
            pl.BlockSpec((GLA_VAL_WIDTH, bn), lambda i, j: (0, j)),
        ],
        out_specs=[pl.BlockSpec((bm, bn), lambda i, j: (i, j)), pl.BlockSpec((MM, bn), _meta_col(nj))],
        out_shape=[jax.ShapeDtypeStruct((MX, D_MODEL), BF16), jax.ShapeDtypeStruct((MM, D_MODEL), BF16)],
        compiler_params=_params(("arbitrary", "arbitrary"), nbytes),
        name="merge",
    )(ofx, ogx, px, px, ofm, ogm, pm, pm, w_o_fox, w_o_gla)


def _outproj_kernel(yx_ref, hx_ref, ym_ref, hm_ref, w_ref, ox_ref, om_ref):
    ox_ref[...] = hx_ref[...] + _dot(yx_ref[...], w_ref[...])

    @pl.when(_is_first_row_block())
    def _():
        om_ref[...] = hm_ref[...] + _dot(ym_ref[...], w_ref[...])


def _outproj(l, yx, hx, ym, hm, w_out):
    bm, bn = OUTPROJ_BM, OUTPROJ_BN
    nj = D_MODEL // bn
    nbytes = 2 * (bm + MM) * D_MODEL * 2 + 2 * D_MODEL * bn * 2 + 4 * (bm + MM) * bn * 4 + bm * bn * 4
    return pl.pallas_call(
        _outproj_kernel,
        grid=(MX // bm, nj),
        in_specs=[
            pl.BlockSpec((bm, D_MODEL), lambda i, j: (i, 0)),
            pl.BlockSpec((bm, bn), lambda i, j: (i, j)),
            pl.BlockSpec((MM, D_MODEL), lambda i, j: (0, 0)),
            pl.BlockSpec((MM, bn), _meta_col(nj)),
            pl.BlockSpec((D_MODEL, bn), lambda i, j: (0, j)),
        ],
        out_specs=[pl.BlockSpec((bm, bn), lambda i, j: (i, j)), pl.BlockSpec((MM, bn), _meta_col(nj))],
        out_shape=[jax.ShapeDtypeStruct((MX, D_MODEL), F32), jax.ShapeDtypeStruct((MM, D_MODEL), F32)],
        compiler_params=_params(("arbitrary", "arbitrary"), nbytes),
        name="outproj",
    )(yx, hx, ym, hm, w_out)


def _mlp_up(xn_ref, w1_ref, u_ref):
    u = jnp.maximum(_dot(xn_ref[...], w1_ref[...]), 0.0)
    u_ref[...] = (u * u).astype(BF16)


def _mlp_step(s, n_slices, xn_ref, w1_ref, w2_ref, u_ref, o_ref):
    @pl.when(s == 0)
    def _():
        _mlp_up(xn_ref, w1_ref, u_ref)

    @pl.when(jnp.logical_and(s > 0, s < n_slices))
    def _():
        down = _dot(u_ref[...], w2_ref[...])
        _mlp_up(xn_ref, w1_ref, u_ref)
        o_ref[...] += down

    @pl.when(s == n_slices)
    def _():
        o_ref[...] += _dot(u_ref[...], w2_ref[...])


def _mlp_kernel(hx_ref, hm_ref, g_ref, w1_ref, w2_ref, ox_ref, om_ref, xn_ref, xnm_ref, u_ref, um_ref):
    s = pl.program_id(1)
    n_slices = pl.num_programs(1) - 1

    @pl.when(s == 0)
    def _():
        _rmsnorm_rows(hx_ref, g_ref, xn_ref, MLP_BM)
        ox_ref[...] = hx_ref[...]

    @pl.when(jnp.logical_and(_is_first_row_block(), s == 0))
    def _():
        _rmsnorm_rows(hm_ref, g_ref, xnm_ref, MM)
        om_ref[...] = hm_ref[...]

    _mlp_step(s, n_slices, xn_ref, w1_ref, w2_ref, u_ref, ox_ref)

    @pl.when(_is_first_row_block())
    def _():
        _mlp_step(s, n_slices, xnm_ref, w1_ref, w2_ref, um_ref, om_ref)


def _mlp_final_kernel(hx_ref, g_ref, w1_ref, w2_ref, fg_ref, ox_ref, xn_ref, u_ref):
    s = pl.program_id(1)
    n_slices = pl.num_programs(1) - 1

    @pl.when(s == 0)
    def _():
        _rmsnorm_rows(hx_ref, g_ref, xn_ref, MLP_BM)
        ox_ref[...] = hx_ref[...]

    _mlp_step(s, n_slices, xn_ref, w1_ref, w2_ref, u_ref, ox_ref)

    @pl.when(s == n_slices)
    def _():
        _rmsnorm_rows(ox_ref, fg_ref, ox_ref, MLP_BM)


def _mlp_bytes():
    bm, bf = MLP_BM, MLP_BF
    return (4 * bm * D_MODEL * 4 + bm * D_MODEL * 2 + 4 * D_MODEL * bf * 2 + bm * bf * 2
            + 4 * MM * D_MODEL * 4 + MM * D_MODEL * 2)


def _mlp_weight_specs():
    bf = MLP_BF
    return [pl.BlockSpec((D_MODEL, bf), lambda i, s: (0, jnp.minimum(s, D_FF // bf - 1))),
            pl.BlockSpec((bf, D_MODEL), lambda i, s: (jnp.maximum(s - 1, 0), 0))]


def _mlp(l, hx, hm, g, w1, w2):
    bm, bf = MLP_BM, MLP_BF
    return pl.pallas_call(
        _mlp_kernel,
        grid=(MX // bm, D_FF // bf + 1),
        in_specs=[
            pl.BlockSpec((bm, D_MODEL), lambda i, s: (i, 0)),
            pl.BlockSpec((MM, D_MODEL), lambda i, s: (0, 0)),
            _layer_spec(l, (1, D_MODEL), lambda i, s: (0, 0)),
            *_mlp_weight_specs(),
        ],
        out_specs=[pl.BlockSpec((bm, D_MODEL), lambda i, s: (i, 0)), pl.BlockSpec((MM, D_MODEL), lambda i, s: (0, 0))],
        out_shape=[jax.ShapeDtypeStruct((MX, D_MODEL), F32), jax.ShapeDtypeStruct((MM, D_MODEL), F32)],
        scratch_shapes=[pltpu.VMEM((bm, D_MODEL), BF16), pltpu.VMEM((MM, D_MODEL), BF16),
                        pltpu.VMEM((bm, bf), BF16), pltpu.VMEM((MM, bf), BF16)],
        compiler_params=_params(("arbitrary", "arbitrary"), _mlp_bytes()),
        name="mlp",
    )(hx, hm, g, w1, w2)


def _mlp_final(l, hx, g, w1, w2, final_g):
    bm, bf = MLP_BM, MLP_BF
    return pl.pallas_call(
        _mlp_final_kernel,
        grid=(MX // bm, D_FF // bf + 1),
        in_specs=[
            pl.BlockSpec((bm, D_MODEL), lambda i, s: (i, 0)),
            _layer_spec(l, (1, D_MODEL), lambda i, s: (0, 0)),
            *_mlp_weight_specs(),
            pl.BlockSpec((1, D_MODEL), lambda i, s: (0, 0)),
        ],
        out_specs=pl.BlockSpec((bm, D_MODEL), lambda i, s: (i, 0)),
        out_shape=jax.ShapeDtypeStruct((MX, D_MODEL), F32),
        scratch_shapes=[pltpu.VMEM((bm, D_MODEL), BF16), pltpu.VMEM((bm, bf), BF16)],
        compiler_params=_params(("arbitrary", "arbitrary"), _mlp_bytes()),
        name="mlp_final",
    )(hx, g, w1, w2, final_g)


WPREP_SEGMENTS = (
    (COL_FQ, 0, FOX_HEAD_DIM ** -0.5 * LOG2E),
    (COL_FK, 0, 1.0),
    (COL_GQ, WIN_GQ0 - COL_GQ, GLA_DK ** -0.5),
    (COL_GK, WIN_GQ0 - COL_GQ, 1.0),
    (COL_GATE_FOX, WIN_GATE0 - COL_GATE_FOX, 1.0),
)


def _tail_slot(row0):
    step = row0 // WPREP_ROWS - 1
    return step, row0 - (step + 1) * WPREP_ROWS


def _wprep_kernel(a_ref, b_ref, o_ref, small_ref):
    i = pl.program_id(1)
    starts = [seg[0] // WPREP_ROWS for seg in WPREP_SEGMENTS] + [N_PROJ // WPREP_ROWS]
    for (_, shift, scale), lo, hi in zip(WPREP_SEGMENTS, starts[:-1], starts[1:]):
        @pl.when(jnp.logical_and(i >= lo, i < hi))
        def _(shift=shift, scale=scale):
            x = a_ref[...] if shift == 0 else jnp.concatenate([a_ref[shift:, :], b_ref[:shift, :]], axis=0)
            o_ref[...] = (x if scale == 1.0 else x * scale).T.astype(BF16)

    @pl.when(i == 0)
    def _():
        small_ref[GLA_RANK + FOX_HEADS:, :] = jnp.zeros((SMALL_W - GLA_RANK - FOX_HEADS, D_MODEL), F32)

    for row0, n, dst in ((WIN_GA0, GLA_RANK, 0), (WIN_FF0, FOX_HEADS, SMALL_FF0)):
        step, off = _tail_slot(row0)

        @pl.when(i == step)
        def _(n=n, dst=dst, off=off):
            small_ref[dst:dst + n, :] = b_ref[off:off + n, :]


def _prep_in_weights(w_in):
    w_t = jnp.swapaxes(w_in, 1, 2)
    rows, tail = WPREP_ROWS, WPREP_TAIL
    for row0, n in ((WIN_GA0, GLA_RANK), (WIN_FF0, FOX_HEADS)):
        assert 0 <= _tail_slot(row0)[1] and _tail_slot(row0)[1] + n <= tail
    nbytes = 2 * (rows + tail) * D_MODEL * 4 + 2 * rows * D_MODEL * 2 + 2 * rows * D_MODEL * 4 + 2 * SMALL_W * D_MODEL * 4
    main, small_t = pl.pallas_call(
        _wprep_kernel,
        grid=(DEPTH, N_PROJ // rows),
        in_specs=[
            pl.BlockSpec((None, rows, D_MODEL), lambda l, i: (l, i, 0)),
            pl.BlockSpec((None, tail, D_MODEL), lambda l, i: (l, (i + 1) * (rows // tail), 0)),
        ],
        out_specs=[
            pl.BlockSpec((None, D_MODEL, rows), lambda l, i: (l, 0, i)),
            pl.BlockSpec((None, SMALL_W, D_MODEL), lambda l, i: (l, 0, 0)),
        ],
        out_shape=[
            jax.ShapeDtypeStruct((DEPTH, D_MODEL, N_PROJ), BF16),
            jax.ShapeDtypeStruct((DEPTH, SMALL_W, D_MODEL), F32),
        ],
        compiler_params=_params(("arbitrary", "arbitrary"), nbytes),
        name="wprep",
    )(w_t, w_t)
    return main, jnp.swapaxes(small_t, 1, 2).astype(BF16)


def kernel(x, meta_tokens, norm_mix_g, w_in, b_forget, w_alpha2, b_alpha, gla_norm_g, w_o_fox, w_o_gla,
           w_out, norm_mlp_g, w_ff1, w_ff2, final_norm_g):
    hx = x.astype(F32).reshape(MX, D_MODEL)
    hm = jnp.pad(meta_tokens.astype(F32), ((0, MM - N_META), (0, 0)))

    w_main, w_small = _prep_in_weights(w_in)
    wa_pad = jnp.pad(w_alpha2, ((0, 0), (0, SMALL_W - GLA_RANK), (0, 0))).astype(BF16)
    norm_mix_g3 = norm_mix_g.reshape(DEPTH, 1, D_MODEL)
    norm_mlp_g3 = norm_mlp_g.reshape(DEPTH, 1, D_MODEL)
    b_alpha3 = b_alpha.reshape(DEPTH, 1, GLA_KEY_WIDTH)
    gla_norm_g3 = gla_norm_g.reshape(DEPTH, 1, GLA_VAL_WIDTH)
    final_g = final_norm_g.reshape(1, D_MODEL)

    for l in range(DEPTH):
        px, smx, ffx, pm, smm, ffm = _inproj(l, hx, hm, norm_mix_g3, w_main, w_small)
        negc = _fox_cumsum(l, b_forget, ffx.reshape(FOX_HEADS, BATCH, SEQ_LANE_ROWS, LANES), ffm)
        ofx, ofm = _fox_attention(px, pm, negc)
        ogx, ogm, w_o_fox_b, w_o_gla_b, w_out_b, w_ff1_b, w_ff2_b = _gla(
            l, px, smx, pm, smm, wa_pad, b_alpha3, gla_norm_g3, (w_o_fox, w_o_gla, w_out, w_ff1, w_ff2))
        yx, ym = _merge(l, ofx, ogx, px, ofm, ogm, pm, w_o_fox_b, w_o_gla_b)
        hx, hm = _outproj(l, yx, hx, ym, hm, w_out_b)
        if l < DEPTH - 1:
            hx, hm = _mlp(l, hx, hm, norm_mlp_g3, w_ff1_b, w_ff2_b)
        else:
            hx = _mlp_final(l, hx, norm_mlp_g3, w_ff1_b, w_ff2_b, final_g)

    return hx.reshape(BATCH, SEQ, D_MODEL)
```

```python
import functools

import jax
import jax.numpy as jnp
from jax import lax
from jax.experimental import pallas as pl
from jax.experimental.pallas import tpu as pltpu

F32 = jnp.float32
BF16 = jnp.bfloat16

D_MODEL = 2048
BATCH = 4
SEQ = 4096
DEPTH = 2
N_META = 16
EPS = 1e-6
MASK_VALUE = -1e30
LOG2E = 1.4426950408889634

FOX_HEADS = 8
FOX_HEAD_DIM = 128
FOX_WIDTH = FOX_HEADS * FOX_HEAD_DIM
GLA_HEADS = 4
GLA_DK = 256
GLA_DV = 512
GLA_KEY_WIDTH = GLA_HEADS * GLA_DK
GLA_VAL_WIDTH = GLA_HEADS * GLA_DV
GLA_RANK = 16
GLA_TAU = 16.0
GLA_CHUNK = 128
GLA_PIVOT = 64
D_FF = 4 * D_MODEL

LANES = 128
SUBLANES = 8
V7X_VMEM_LIMIT_CAP = 60 * 1024 * 1024

MX = BATCH * SEQ
MM = LANES
SEQ_LANE_ROWS = SEQ // LANES
NC_META_ROW = SEQ_LANE_ROWS
NC_ROWS = SEQ_LANE_ROWS + SUBLANES

COL_FQ = 0
COL_FK = COL_FQ + FOX_WIDTH
COL_FV = COL_FK + FOX_WIDTH
COL_GQ = COL_FV + FOX_WIDTH
COL_GK = COL_GQ + GLA_KEY_WIDTH
COL_GV = COL_GK + GLA_KEY_WIDTH
COL_GR = COL_GV + GLA_VAL_WIDTH
COL_GATE_FOX = COL_GR + GLA_VAL_WIDTH
COL_GATE_GLA = COL_GATE_FOX + D_MODEL
N_PROJ = COL_GATE_GLA + D_MODEL
WIN_FF0 = 3 * FOX_WIDTH
WIN_GQ0 = WIN_FF0 + FOX_HEADS
WIN_GA0 = WIN_GQ0 + 2 * GLA_KEY_WIDTH + 2 * GLA_VAL_WIDTH
WIN_GATE0 = WIN_GA0 + GLA_RANK
SMALL_W = LANES
SMALL_FF0 = GLA_RANK

NORM_CHUNK = 128
INPROJ_BM, INPROJ_BN = 1024, 1024
MERGE_BM, MERGE_BN = 1024, 1024
OUTPROJ_BM, OUTPROJ_BN = 1024, 1024
MLP_BM, MLP_BF = 1024, 512
FOX_NH = 4
FOX_BQ = 512
FOX_BK = 512
FOX_STRIP = 64
GLA_RB = 512
GLA_N_CASTS = 5
WPREP_ROWS = 512
WPREP_TAIL = 32


def _params(sem, nbytes, slack=4 << 20):
    return pltpu.CompilerParams(dimension_semantics=sem,
                                vmem_limit_bytes=int(min(V7X_VMEM_LIMIT_CAP, nbytes + slack)))


def _layer_spec(l, shape, index_map):
    return pl.BlockSpec((None,) + shape, lambda *idx: (l,) + index_map(*idx))


def _meta_col(nj):
    return lambda i, j: (0, jnp.where(i == 0, j, nj - 1))


def _rmsnorm_rows(src_ref, g_ref, dst_ref, rows, chunk=NORM_CHUNK):
    g = g_ref[...]

    def body(c, _):
        r0 = pl.multiple_of(c * chunk, chunk)
        x = src_ref[pl.ds(r0, chunk), :]
        ms = jnp.mean(x * x, axis=-1, keepdims=True)
        dst_ref[pl.ds(r0, chunk), :] = ((x * lax.rsqrt(ms + EPS)) * g).astype(dst_ref.dtype)
        return 0

    lax.fori_loop(0, rows // chunk, body, 0)


def _log_sigmoid(x):
    return jnp.minimum(x, 0.0) - jnp.log(1.0 + jnp.exp(-jnp.abs(x)))


def _sigmoid(x):
    return 1.0 / (1.0 + jnp.exp(-x))


def _dot(a, b):
    return jnp.dot(a, b, preferred_element_type=F32)


def _dot_nt(a, b):
    return lax.dot_general(a, b, (((1,), (1,)), ((), ())), preferred_element_type=F32)


def _dot_tn(a, b):
    return lax.dot_general(a, b, (((0,), (0,)), ((), ())), preferred_element_type=F32)


def _split_bf16(x, n):
    pieces = []
    rem = x
    for _ in range(n):
        hi = rem.astype(BF16)
        pieces.append(hi)
        rem = rem - hi.astype(F32)
    return pieces


def _dot_f32_lhs(x, r, n):
    acc = None
    for p in _split_bf16(x, n):
        t = _dot(p, r)
        acc = t if acc is None else acc + t
    return acc


def _dot_f32_rhs(l, x, n):
    acc = None
    for p in _split_bf16(x, n):
        t = _dot(l, p)
        acc = t if acc is None else acc + t
    return acc


def _is_first_row_block():
    return pl.program_id(0) == 0


def _inproj_kernel(hx_ref, hm_ref, g_ref, w_ref, ws_ref,
                   px_ref, smx_ref, ffx_ref, pm_ref, smm_ref, ffm_ref, xn_ref, xnm_ref):
    j = pl.program_id(1)
    ff_rows = slice(SMALL_FF0, SMALL_FF0 + FOX_HEADS)

    @pl.when(j == 0)
    def _():
        _rmsnorm_rows(hx_ref, g_ref, xn_ref, INPROJ_BM)
        sm = _dot(xn_ref[...], ws_ref[...])
        smx_ref[...] = sm
        ffx_ref[...] = sm.T[ff_rows, :]

    @pl.when(jnp.logical_and(_is_first_row_block(), j == 0))
    def _():
        _rmsnorm_rows(hm_ref, g_ref, xnm_ref, MM)
        sm = _dot(xnm_ref[...], ws_ref[...])
        smm_ref[...] = sm
        ffm_ref[...] = sm.T[ff_rows, :]

    px_ref[...] = _dot(xn_ref[...], w_ref[...]).astype(BF16)

    @pl.when(_is_first_row_block())
    def _():
        pm_ref[...] = _dot(xnm_ref[...], w_ref[...]).astype(BF16)


def _inproj(l, hx, hm, g, w_main, w_small):
    bm, bn = INPROJ_BM, INPROJ_BN
    nj = N_PROJ // bn
    nbytes = (2 * bm * D_MODEL * 4 + 2 * D_MODEL * bn * 2 + 2 * D_MODEL * SMALL_W * 2
              + 2 * bm * bn * 2 + 2 * bm * SMALL_W * 4 + bm * D_MODEL * 2 + bm * bn * 4 + 2 * 8 * bm * 4
              + 2 * MM * D_MODEL * 4 + 2 * MM * bn * 2 + MM * D_MODEL * 2)
    return pl.pallas_call(
        _inproj_kernel,
        grid=(MX // bm, nj),
        in_specs=[
            pl.BlockSpec((bm, D_MODEL), lambda i, j: (i, 0)),
            pl.BlockSpec((MM, D_MODEL), lambda i, j: (0, 0)),
            _layer_spec(l, (1, D_MODEL), lambda i, j: (0, 0)),
            _layer_spec(l, (D_MODEL, bn), lambda i, j: (0, j)),
            _layer_spec(l, (D_MODEL, SMALL_W), lambda i, j: (0, 0)),
        ],
        out_specs=[
            pl.BlockSpec((bm, bn), lambda i, j: (i, j)),
            pl.BlockSpec((bm, SMALL_W), lambda i, j: (i, 0)),
            pl.BlockSpec((FOX_HEADS, bm), lambda i, j: (0, i)),
            pl.BlockSpec((MM, bn), _meta_col(nj)),
            pl.BlockSpec((MM, SMALL_W), lambda i, j: (0, 0)),
            pl.BlockSpec((FOX_HEADS, MM), lambda i, j: (0, 0)),
        ],
        out_shape=[
            jax.ShapeDtypeStruct((MX, N_PROJ), BF16),
            jax.ShapeDtypeStruct((MX, SMALL_W), F32),
            jax.ShapeDtypeStruct((FOX_HEADS, MX), F32),
            jax.ShapeDtypeStruct((MM, N_PROJ), BF16),
            jax.ShapeDtypeStruct((MM, SMALL_W), F32),
            jax.ShapeDtypeStruct((FOX_HEADS, MM), F32),
        ],
        scratch_shapes=[pltpu.VMEM((bm, D_MODEL), BF16), pltpu.VMEM((MM, D_MODEL), BF16)],
        compiler_params=_params(("arbitrary", "arbitrary"), nbytes),
        name="inproj",
    )(hx, hm, g, w_main, w_small)


def _fox_cumsum_kernel(bf_ref, ffx_ref, ffm_ref, nc_ref, *, layer):
    li = lax.broadcasted_iota(jnp.int32, (LANES, LANES), 0)
    lj = lax.broadcasted_iota(jnp.int32, (LANES, LANES), 1)
    upper = jnp.where(li <= lj, 1.0, 0.0).astype(BF16)
    ones = jnp.ones((LANES, LANES), BF16)
    ri = lax.broadcasted_iota(jnp.int32, (SEQ_LANE_ROWS, SEQ_LANE_ROWS), 0)
    rj = lax.broadcasted_iota(jnp.int32, (SEQ_LANE_ROWS, SEQ_LANE_ROWS), 1)
    strict_lower = jnp.where(rj < ri, 1.0, 0.0).astype(BF16)
    head_id = lax.broadcasted_iota(jnp.int32, (FOX_HEADS, LANES), 0)
    lane_id = lax.broadcasted_iota(jnp.int32, (FOX_HEADS, LANES), 1)
    bias = jnp.zeros((FOX_HEADS, LANES), F32)
    for hd in range(FOX_HEADS):
        bias = jnp.where(head_id == hd, bf_ref[layer, hd], bias)
    real = lane_id < N_META
    lf_m = jnp.where(real, _log_sigmoid(ffm_ref[...] + bias), 0.0)
    cum_m = _dot_f32_lhs(lf_m, upper, 3)
    tot_m = _dot_f32_lhs(lf_m, ones, 3)
    nc_meta = jnp.where(real, cum_m * (-LOG2E), MASK_VALUE)
    group_row = lax.broadcasted_iota(jnp.int32, (SUBLANES, LANES), 0)
    for hd in range(FOX_HEADS):
        lf = _log_sigmoid(ffx_ref[hd, 0] + bf_ref[layer, hd])
        c = (_dot_f32_lhs(lf, upper, 3) + _dot_f32_rhs(strict_lower, _dot_f32_lhs(lf, ones, 3), 3)
             + tot_m[hd:hd + 1, :])
        nc_ref[0, hd, :SEQ_LANE_ROWS, :] = c * (-LOG2E)
        nc_ref[0, hd, SEQ_LANE_ROWS:, :] = jnp.where(
            group_row == 0, jnp.broadcast_to(nc_meta[hd:hd + 1, :], (SUBLANES, LANES)), 0.0)


def _fox_cumsum(l, b_forget, ffx_rows, ffm):
    return pl.pallas_call(
        functools.partial(_fox_cumsum_kernel, layer=l),
        grid=(BATCH,),
        in_specs=[
            pl.BlockSpec(memory_space=pltpu.SMEM),
            pl.BlockSpec((FOX_HEADS, 1, SEQ_LANE_ROWS, LANES), lambda b: (0, b, 0, 0)),
            pl.BlockSpec((FOX_HEADS, MM), lambda b: (0, 0)),
        ],
        out_specs=pl.BlockSpec((1, FOX_HEADS, NC_ROWS, LANES), lambda b: (b, 0, 0, 0)),
        out_shape=jax.ShapeDtypeStruct((BATCH, FOX_HEADS, NC_ROWS, LANES), F32),
        compiler_params=_params(("arbitrary",), 8 << 20),
        name="fox_cumsum",
    )(b_forget, ffx_rows, ffm)


def _fox_kernel(qx_ref, kx_ref, vx_ref, qm_ref, km_ref, vm_ref, nc_ref, ox_ref, om_ref,
                vxs_ref, vms_ref, s_ref, p_ref, m_ref, al_ref, acc_ref):
    hd_cols = [slice(hh * FOX_HEAD_DIM, (hh + 1) * FOX_HEAD_DIM) for hh in range(FOX_NH)]
    for hh in range(FOX_NH):
        vxs_ref[hh, :, :FOX_HEAD_DIM] = vx_ref[:, hd_cols[hh]]
        vxs_ref[hh, :, FOX_HEAD_DIM:] = jnp.ones((SEQ, FOX_HEAD_DIM), BF16)
        vms_ref[hh, :, :FOX_HEAD_DIM] = vm_ref[:, hd_cols[hh]]
        vms_ref[hh, :, FOX_HEAD_DIM:] = jnp.ones((MM, FOX_HEAD_DIM), BF16)
    lane_chunks = FOX_BK // LANES
    row_id = lax.broadcasted_iota(jnp.int32, (FOX_BQ, FOX_BK), 0)
    col_id = lax.broadcasted_iota(jnp.int32, (FOX_BQ, FOX_BK), 1)
    causal = col_id <= row_id

    def meta_bias(hh):
        return nc_ref[0, hh, NC_META_ROW:NC_META_ROW + 1, :]

    for hh in range(FOX_NH):
        s = _dot_nt(qm_ref[:, hd_cols[hh]], km_ref[:, hd_cols[hh]]) + meta_bias(hh)
        s = jnp.where(causal[:MM, :MM], s, MASK_VALUE)
        p = jnp.exp2(s - jnp.max(s, axis=1, keepdims=True)).astype(BF16)
        acc = _dot(p, vms_ref[hh])
        om_ref[:, hd_cols[hh]] = (acc[:, :FOX_HEAD_DIM] / acc[:, FOX_HEAD_DIM:]).astype(BF16)

    def q_body(qi, _):
        q0 = pl.multiple_of(qi * FOX_BQ, FOX_BQ)
        for hh in range(FOX_NH):
            s = _dot_nt(qx_ref[pl.ds(q0, FOX_BQ), hd_cols[hh]], km_ref[:, hd_cols[hh]]) + meta_bias(hh)
            m0 = jnp.max(s, axis=1, keepdims=True)
            m_ref[hh] = jnp.broadcast_to(m0, (FOX_BQ, LANES))
            acc_ref[hh] = _dot(jnp.exp2(s - m0).astype(BF16), vms_ref[hh])

        def tile(kj, diagonal):
            k0 = pl.multiple_of(kj * FOX_BK, FOX_BK)
            for hh in range(FOX_NH):
                s = _dot_nt(qx_ref[pl.ds(q0, FOX_BQ), hd_cols[hh]], kx_ref[pl.ds(k0, FOX_BK), hd_cols[hh]])
                s = s + jnp.concatenate(
                    [nc_ref[0, hh, pl.ds(kj * lane_chunks + c, 1), :] for c in range(lane_chunks)], axis=1)
                s_ref[hh] = jnp.where(causal, s, MASK_VALUE) if diagonal else s
            for hh in range(FOX_NH):
                for r in range(0, FOX_BQ, FOX_STRIP):
                    rows = slice(r, r + FOX_STRIP)
                    sv = s_ref[hh, rows, :]
                    m_old = m_ref[hh, rows, :]
                    m_new = jnp.maximum(m_old, jnp.max(sv, axis=1, keepdims=True))
                    m_ref[hh, rows, :] = m_new
                    al_ref[hh, rows, :] = jnp.exp2(m_old - m_new)
                    p_ref[hh, rows, :] = jnp.exp2(sv - jnp.concatenate([m_new] * lane_chunks, axis=1)).astype(BF16)
            for hh in range(FOX_NH):
                alpha = jnp.concatenate([al_ref[hh]] * 2, axis=1)
                acc_ref[hh] = alpha * acc_ref[hh] + _dot(p_ref[hh], vxs_ref[hh, pl.ds(k0, FOX_BK), :])

        def kv_pair(pj, _):
            tile(2 * pj, False)
            tile(2 * pj + 1, False)
            return 0

        lax.fori_loop(0, qi // 2, kv_pair, 0)

        @pl.when(lax.rem(qi, 2) == 1)
        def _():
            tile(qi - 1, False)

        tile(qi, True)
        for hh in range(FOX_NH):
            acc = acc_ref[hh]
            ox_ref[pl.ds(q0, FOX_BQ), hd_cols[hh]] = (
                acc[:, :FOX_HEAD_DIM] / acc[:, FOX_HEAD_DIM:]).astype(BF16)
        return 0

    lax.fori_loop(0, SEQ // FOX_BQ, q_body, 0)


def _fox_attention(px, pm, negc):
    w = FOX_NH * FOX_HEAD_DIM
    xblk, mblk = (SEQ, w), (MM, w)
    nbytes = (2 * 4 * SEQ * w * 2 + 2 * 4 * MM * w * 2 + FOX_NH * (SEQ + MM) * 2 * FOX_HEAD_DIM * 2
              + FOX_NH * FOX_BQ * (FOX_BK * 6 + 2 * LANES * 4 + 2 * FOX_HEAD_DIM * 4)
              + 2 * FOX_NH * NC_ROWS * LANES * 4)
    return pl.pallas_call(
        _fox_kernel,
        grid=(BATCH, FOX_HEADS // FOX_NH),
        in_specs=[
            pl.BlockSpec(xblk, lambda b, g: (b, COL_FQ // w + g)),
            pl.BlockSpec(xblk, lambda b, g: (b, COL_FK // w + g)),
            pl.BlockSpec(xblk, lambda b, g: (b, COL_FV // w + g)),
            pl.BlockSpec(mblk, lambda b, g: (0, COL_FQ // w + g)),
            pl.BlockSpec(mblk, lambda b, g: (0, COL_FK // w + g)),
            pl.BlockSpec(mblk, lambda b, g: (0, COL_FV // w + g)),
            pl.BlockSpec((1, FOX_NH, NC_ROWS, LANES), lambda b, g: (b, g, 0, 0)),
        ],
        out_specs=[pl.BlockSpec(xblk, lambda b, g: (b, g)), pl.BlockSpec((None,) + mblk, lambda b, g: (b, 0, g))],
        out_shape=[jax.ShapeDtypeStruct((MX, FOX_WIDTH), BF16),
                   jax.ShapeDtypeStruct((BATCH, MM, FOX_WIDTH), BF16)],
        scratch_shapes=[
            pltpu.VMEM((FOX_NH, SEQ, 2 * FOX_HEAD_DIM), BF16),
            pltpu.VMEM((FOX_NH, MM, 2 * FOX_HEAD_DIM), BF16),
            pltpu.VMEM((FOX_NH, FOX_BQ, FOX_BK), F32),
            pltpu.VMEM((FOX_NH, FOX_BQ, FOX_BK), BF16),
            pltpu.VMEM((FOX_NH, FOX_BQ, LANES), F32),
            pltpu.VMEM((FOX_NH, FOX_BQ, LANES), F32),
            pltpu.VMEM((FOX_NH, FOX_BQ, 2 * FOX_HEAD_DIM), F32),
        ],
        compiler_params=_params(("arbitrary", "arbitrary"), nbytes),
        name="fox_attn",
    )(px, px, px, pm, pm, pm, negc)


def _gla_kernel(*refs):
    nh = GLA_HEADS
    n_in = 9 + 4 * nh
    for src_ref, dst_ref in zip(refs[n_in:n_in + GLA_N_CASTS], refs[n_in + GLA_N_CASTS + 2:n_in + 2 * GLA_N_CASTS + 2]):
        dst_ref[...] = src_ref[...].astype(BF16)
    qx_ref, kx_ref = refs[0], refs[1]
    vx_refs, rx_refs = refs[2:2 + nh], refs[2 + nh:2 + 2 * nh]
    smx_ref = refs[2 + 2 * nh]
    qm_ref, km_ref = refs[3 + 2 * nh], refs[4 + 2 * nh]
    vm_refs, rm_refs = refs[5 + 2 * nh:5 + 3 * nh], refs[5 + 3 * nh:5 + 4 * nh]
    smm_ref, wa_ref, ba_ref, gn_ref = refs[5 + 4 * nh:n_in]
    ox_ref, om_ref = refs[n_in + GLA_N_CASTS:n_in + GLA_N_CASTS + 2]
    st_ref, b_ref, bm_ref = refs[-3:]
    C = GLA_CHUNK

    ci = lax.broadcasted_iota(jnp.int32, (C, C), 0)
    cj = lax.broadcasted_iota(jnp.int32, (C, C), 1)
    tril = cj <= ci
    tril_ones = jnp.where(tril, 1.0, 0.0).astype(BF16)

    def log2_decay(small):
        z = _dot(small.astype(BF16), wa_ref[...]) + ba_ref[...]
        return _log_sigmoid(z) * (LOG2E / GLA_TAU)

    def chunk(hd, b, qc, kc, vc, gr):
        vs = slice(hd * GLA_DV, (hd + 1) * GLA_DV)
        b_mid = b[GLA_PIVOT - 1:GLA_PIVOT, :]
        b_last = b[C - 1:C, :]
        q_abs = (qc * jnp.exp2(b)).astype(BF16)
        q_dec = (qc * jnp.exp2(b - b_mid)).astype(BF16)
        k_inv = (kc * jnp.exp2(b_mid - b)).astype(BF16)
        k_end = (kc * jnp.exp2(b_last - b)).astype(BF16)
        a = jnp.where(tril, _dot_nt(q_dec, k_inv), 0.0).astype(BF16)
        st = st_ref[hd]
        o = _dot(a, vc) + _dot_nt(q_abs, st.astype(BF16))
        st_ref[hd] = st * jnp.exp2(b_last) + _dot_tn(vc, k_end)
        rms = lax.rsqrt(jnp.mean(o * o, axis=-1, keepdims=True) + EPS)
        return ((o * rms) * gn_ref[:, vs] * (gr * _sigmoid(gr))).astype(BF16)

    @pl.when(pl.program_id(1) == 0)
    def _():
        st_ref[...] = jnp.zeros_like(st_ref)
        real = lax.broadcasted_iota(jnp.int32, (MM, 1), 0) < N_META
        bm_ref[...] = _dot_f32_rhs(tril_ones, jnp.where(real, log2_decay(smm_ref[...]), 0.0), 2)
        for hd in range(nh):
            ks = slice(hd * GLA_DK, (hd + 1) * GLA_DK)
            vs = slice(hd * GLA_DV, (hd + 1) * GLA_DV)
            kc = jnp.where(real, km_ref[:, ks].astype(F32), 0.0)
            om_ref[:, vs] = chunk(hd, bm_ref[:, ks], qm_ref[:, ks].astype(F32), kc, vm_refs[hd][...],
                                  rm_refs[hd][...].astype(F32))

    g = log2_decay(smx_ref[...])
    for c in range(GLA_RB // C):
        b_ref[c * C:(c + 1) * C, :] = _dot_f32_rhs(tril_ones, g[c * C:(c + 1) * C, :], 2)

    def chunk_body(c, _):
        r0 = pl.multiple_of(c * C, C)
        for hd in range(nh):
            ks = slice(hd * GLA_DK, (hd + 1) * GLA_DK)
            vs = slice(hd * GLA_DV, (hd + 1) * GLA_DV)
            ox_ref[pl.ds(r0, C), vs] = chunk(
                hd, b_ref[pl.ds(r0, C), ks], qx_ref[pl.ds(r0, C), ks].astype(F32),
                kx_ref[pl.ds(r0, C), ks].astype(F32), vx_refs[hd][pl.ds(r0, C), :],
                rx_refs[hd][pl.ds(r0, C), :].astype(F32))
        return 0

    lax.fori_loop(0, GLA_RB // C, chunk_body, 0)


def _gla(l, px, smx, pm, smm, wa_pad, b_alpha, gn_g, riders):
    rb = GLA_RB
    nt = SEQ // rb
    row = lambda b, t: b * nt + t
    steps = BATCH * nt
    slab = lambda b, t: (row(b, t), 0)
    slabs = [(w.shape[1] // steps, w.shape[2]) for w in riders]
    nbytes = (2 * rb * (2 * GLA_KEY_WIDTH + 2 * GLA_VAL_WIDTH) * 2 + 2 * rb * SMALL_W * 4
              + 2 * rb * GLA_VAL_WIDTH * 2 + GLA_HEADS * GLA_DV * GLA_DK * 4 + 4 * rb * GLA_KEY_WIDTH * 4
              + 2 * SMALL_W * GLA_KEY_WIDTH * 2
              + 2 * MM * (2 * GLA_KEY_WIDTH + 3 * GLA_VAL_WIDTH) * 2 + 4 * MM * GLA_KEY_WIDTH * 4)

    def x_heads(col0):
        return [pl.BlockSpec((rb, GLA_DV), functools.partial(lambda b, t, c: (row(b, t), c), c=col0 // GLA_DV + hd))
                for hd in range(GLA_HEADS)]

    def m_heads(col0):
        return [pl.BlockSpec((MM, GLA_DV), functools.partial(lambda b, t, c: (0, c), c=col0 // GLA_DV + hd))
                for hd in range(GLA_HEADS)]

    return pl.pallas_call(
        _gla_kernel,
        grid=(BATCH, nt),
        in_specs=[
            pl.BlockSpec((rb, GLA_KEY_WIDTH), lambda b, t: (row(b, t), COL_GQ // GLA_KEY_WIDTH)),
            pl.BlockSpec((rb, GLA_KEY_WIDTH), lambda b, t: (row(b, t), COL_GK // GLA_KEY_WIDTH)),
            *x_heads(COL_GV),
            *x_heads(COL_GR),
            pl.BlockSpec((rb, SMALL_W), lambda b, t: (row(b, t), 0)),
            pl.BlockSpec((MM, GLA_KEY_WIDTH), lambda b, t: (0, COL_GQ // GLA_KEY_WIDTH)),
            pl.BlockSpec((MM, GLA_KEY_WIDTH), lambda b, t: (0, COL_GK // GLA_KEY_WIDTH)),
            *m_heads(COL_GV),
            *m_heads(COL_GR),
            pl.BlockSpec((MM, SMALL_W), lambda b, t: (0, 0)),
            _layer_spec(l, (SMALL_W, GLA_KEY_WIDTH), lambda b, t: (0, 0)),
            _layer_spec(l, (1, GLA_KEY_WIDTH), lambda b, t: (0, 0)),
            _layer_spec(l, (1, GLA_VAL_WIDTH), lambda b, t: (0, 0)),
            *[_layer_spec(l, sh, slab) for sh in slabs],
        ],
        out_specs=[
            pl.BlockSpec((rb, GLA_VAL_WIDTH), lambda b, t: (row(b, t), 0)),
            pl.BlockSpec((MM, GLA_VAL_WIDTH), lambda b, t: (0, 0)),
            *[pl.BlockSpec(sh, slab) for sh in slabs],
        ],
        out_shape=[jax.ShapeDtypeStruct((MX, GLA_VAL_WIDTH), BF16), jax.ShapeDtypeStruct((MM, GLA_VAL_WIDTH), BF16),
                   *[jax.ShapeDtypeStruct(w.shape[1:], BF16) for w in riders]],
        scratch_shapes=[
            pltpu.VMEM((GLA_HEADS, GLA_DV, GLA_DK), F32),
            pltpu.VMEM((rb, GLA_KEY_WIDTH), F32),
            pltpu.VMEM((MM, GLA_KEY_WIDTH), F32),
        ],
        compiler_params=_params(("arbitrary", "arbitrary"), nbytes + sum(12 * r * c for r, c in slabs)),
        name="gla",
    )(*([px] * (2 + 2 * GLA_HEADS)), smx, *([pm] * (2 + 2 * GLA_HEADS)), smm, wa_pad, b_alpha, gn_g, *riders)


def _merge_rows(of_ref, og_ref, wf_ref, wg_ref, gf_ref, gg_ref, y_ref):
    t_fox = _dot(of_ref[...], wf_ref[...])
    t_gla = _dot(og_ref[...], wg_ref[...])
    y = _sigmoid(gf_ref[...].astype(F32)) * t_fox + _sigmoid(gg_ref[...].astype(F32)) * t_gla
    y_ref[...] = y.astype(BF16)


def _merge_kernel(ofx_ref, ogx_ref, gfx_ref, ggx_ref, ofm_ref, ogm_ref, gfm_ref, ggm_ref, wf_ref, wg_ref,
                  yx_ref, ym_ref):
    _merge_rows(ofx_ref, ogx_ref, wf_ref, wg_ref, gfx_ref, ggx_ref, yx_ref)

    @pl.when(_is_first_row_block())
    def _():
        _merge_rows(ofm_ref, ogm_ref, wf_ref, wg_ref, gfm_ref, ggm_ref, ym_ref)


def _merge(l, ofx, ogx, px, ofm, ogm, pm, w_o_fox, w_o_gla):
    bm, bn = MERGE_BM, MERGE_BN
    nj = D_MODEL // bn
    nbytes = (2 * (bm + MM) * (FOX_WIDTH + GLA_VAL_WIDTH) * 2 + 2 * (FOX_WIDTH + GLA_VAL_WIDTH) * bn * 2
              + 6 * (bm + MM) * bn * 2 + 4 * bm * bn * 4)
    gate = lambda col0: (lambda i, j: (i, col0 // bn + j))
    mgate = lambda col0: (lambda i, j: (0, col0 // bn + jnp.where(i == 0, j, nj - 1)))
    return pl.pallas_call(
        _merge_kernel,
        grid=(MX // bm, nj),
        in_specs=[
            pl.BlockSpec((bm, FOX_WIDTH), lambda i, j: (i, 0)),
            pl.BlockSpec((bm, GLA_VAL_WIDTH), lambda i, j: (i, 0)),
            pl.BlockSpec((bm, bn), gate(COL_GATE_FOX)),
            pl.BlockSpec((bm, bn), gate(COL_GATE_GLA)),
            pl.BlockSpec((None, MM, FOX_WIDTH), lambda i, j: (0, 0, 0)),
            pl.BlockSpec((MM, GLA_VAL_WIDTH), lambda i, j: (0, 0)),
            pl.BlockSpec((MM, bn), mgate(COL_GATE_FOX)),
            pl.BlockSpec((MM, bn), mgate(COL_GATE_GLA)),
            pl.BlockSpec((FOX_WIDTH, bn), lambda i, j: (0, j)),
            pl.BlockSpec((GLA_VAL_WIDTH, bn), lambda i, j: (0, j)),
        ],
        out_specs=[pl.BlockSpec((bm, bn), lambda i, j: (i, j)), pl.BlockSpec((MM, bn), _meta_col(nj))],
        out_shape=[jax.ShapeDtypeStruct((MX, D_MODEL), BF16), jax.ShapeDtypeStruct((MM, D_MODEL), BF16)],
        compiler_params=_params(("arbitrary", "arbitrary"), nbytes),
        name="merge",
    )(ofx, ogx, px, px, ofm, ogm, pm, pm, w_o_fox, w_o_gla)


def _outproj_kernel(yx_ref, hx_ref, ym_ref, hm_ref, w_ref, ox_ref, om_ref):
    ox_ref[...] = hx_ref[...] + _dot(yx_ref[...], w_ref[...])

    @pl.when(_is_first_row_block())
    def _():
        om_ref[...] = hm_ref[...] + _dot(ym_ref[...], w_ref[...])


def _outproj(l, yx, hx, ym, hm, w_out):
    bm, bn = OUTPROJ_BM, OUTPROJ_BN
    nj = D_MODEL // bn
    nbytes = 2 * (bm + MM) * D_MODEL * 2 + 2 * D_MODEL * bn * 2 + 4 * (bm + MM) * bn * 4 + bm * bn * 4
    return pl.pallas_call(
        _outproj_kernel,
        grid=(MX // bm, nj),
        in_specs=[
            pl.BlockSpec((bm, D_MODEL), lambda i, j: (i, 0)),
            pl.BlockSpec((bm, bn), lambda i, j: (i, j)),
            pl.BlockSpec((MM, D_MODEL), lambda i, j: (0, 0)),
            pl.BlockSpec((MM, bn), _meta_col(nj)),
            pl.BlockSpec((D_MODEL, bn), lambda i, j: (0, j)),
        ],
        out_specs=[pl.BlockSpec((bm, bn), lambda i, j: (i, j)), pl.BlockSpec((MM, bn), _meta_col(nj))],
        out_shape=[jax.ShapeDtypeStruct((MX, D_MODEL), F32), jax.ShapeDtypeStruct((MM, D_MODEL), F32)],
        compiler_params=_params(("arbitrary", "arbitrary"), nbytes),
        name="outproj",
    )(yx, hx, ym, hm, w_out)


def _mlp_step(xn_ref, w1_ref, w2_ref, o_ref):
    u = jnp.maximum(_dot(xn_ref[...], w1_ref[...]), 0.0)
    o_ref[...] += _dot((u * u).astype(BF16), w2_ref[...])


def _mlp_kernel(hx_ref, hm_ref, g_ref, w1_ref, w2_ref, ox_ref, om_ref, xn_ref, xnm_ref):
    j = pl.program_id(1)

    @pl.when(j == 0)
    def _():
        _rmsnorm_rows(hx_ref, g_ref, xn_ref, MLP_BM)
        ox_ref[...] = hx_ref[...]

    @pl.when(jnp.logical_and(_is_first_row_block(), j == 0))
    def _():
        _rmsnorm_rows(hm_ref, g_ref, xnm_ref, MM)
        om_ref[...] = hm_ref[...]

    _mlp_step(xn_ref, w1_ref, w2_ref, ox_ref)

    @pl.when(_is_first_row_block())
    def _():
        _mlp_step(xnm_ref, w1_ref, w2_ref, om_ref)


def _mlp_final_kernel(hx_ref, g_ref, w1_ref, w2_ref, fg_ref, ox_ref, xn_ref):
    j = pl.program_id(1)

    @pl.when(j == 0)
    def _():
        _rmsnorm_rows(hx_ref, g_ref, xn_ref, MLP_BM)
        ox_ref[...] = hx_ref[...]

    _mlp_step(xn_ref, w1_ref, w2_ref, ox_ref)

    @pl.when(j == pl.num_programs(1) - 1)
    def _():
        _rmsnorm_rows(ox_ref, fg_ref, ox_ref, MLP_BM)


def _mlp_bytes(with_meta):
    bm, bf = MLP_BM, MLP_BF
    meta = 4 * MM * D_MODEL * 4 + MM * D_MODEL * 2 if with_meta else 0
    return 4 * bm * D_MODEL * 4 + bm * D_MODEL * 2 + 4 * D_MODEL * bf * 2 + bm * bf * 6 + meta


def _mlp(l, hx, hm, g, w1, w2):
    bm, bf = MLP_BM, MLP_BF
    return pl.pallas_call(
        _mlp_kernel,
        grid=(MX // bm, D_FF // bf),
        in_specs=[
            pl.BlockSpec((bm, D_MODEL), lambda i, j: (i, 0)),
            pl.BlockSpec((MM, D_MODEL), lambda i, j: (0, 0)),
            _layer_spec(l, (1, D_MODEL), lambda i, j: (0, 0)),
            pl.BlockSpec((D_MODEL, bf), lambda i, j: (0, j)),
            pl.BlockSpec((bf, D_MODEL), lambda i, j: (j, 0)),
        ],
        out_specs=[pl.BlockSpec((bm, D_MODEL), lambda i, j: (i, 0)), pl.BlockSpec((MM, D_MODEL), lambda i, j: (0, 0))],
        out_shape=[jax.ShapeDtypeStruct((MX, D_MODEL), F32), jax.ShapeDtypeStruct((MM, D_MODEL), F32)],
        scratch_shapes=[pltpu.VMEM((bm, D_MODEL), BF16), pltpu.VMEM((MM, D_MODEL), BF16)],
        compiler_params=_params(("arbitrary", "arbitrary"), _mlp_bytes(True)),
        name="mlp",
    )(hx, hm, g, w1, w2)


def _mlp_final(l, hx, g, w1, w2, final_g):
    bm, bf = MLP_BM, MLP_BF
    return pl.pallas_call(
        _mlp_final_kernel,
        grid=(MX // bm, D_FF // bf),
        in_specs=[
            pl.BlockSpec((bm, D_MODEL), lambda i, j: (i, 0)),
            _layer_spec(l, (1, D_MODEL), lambda i, j: (0, 0)),
            pl.BlockSpec((D_MODEL, bf), lambda i, j: (0, j)),
            pl.BlockSpec((bf, D_MODEL), lambda i, j: (j, 0)),
            pl.BlockSpec((1, D_MODEL), lambda i, j: (0, 0)),
        ],
        out_specs=pl.BlockSpec((bm, D_MODEL), lambda i, j: (i, 0)),
        out_shape=jax.ShapeDtypeStruct((MX, D_MODEL), F32),
        scratch_shapes=[pltpu.VMEM((bm, D_MODEL), BF16)],
        compiler_params=_params(("arbitrary", "arbitrary"), _mlp_bytes(False)),
        name="mlp_final",
    )(hx, g, w1, w2, final_g)


WPREP_SEGMENTS = (
    (COL_FQ, 0, FOX_HEAD_DIM ** -0.5 * LOG2E),
    (COL_FK, 0, 1.0),
    (COL_GQ, WIN_GQ0 - COL_GQ, GLA_DK ** -0.5),
    (COL_GK, WIN_GQ0 - COL_GQ, 1.0),
    (COL_GATE_FOX, WIN_GATE0 - COL_GATE_FOX, 1.0),
)


def _tail_slot(row0):
    step = row0 // WPREP_ROWS - 1
    return step, row0 - (step + 1) * WPREP_ROWS


def _wprep_kernel(a_ref, b_ref, o_ref, small_ref):
    i = pl.program_id(1)
    starts = [seg[0] // WPREP_ROWS for seg in WPREP_SEGMENTS] + [N_PROJ // WPREP_ROWS]
    for (_, shift, scale), lo, hi in zip(WPREP_SEGMENTS, starts[:-1], starts[1:]):
        @pl.when(jnp.logical_and(i >= lo, i < hi))
        def _(shift=shift, scale=scale):
            x = a_ref[...] if shift == 0 else jnp.concatenate([a_ref[shift:, :], b_ref[:shift, :]], axis=0)
            o_ref[...] = (x if scale == 1.0 else x * scale).T.astype(BF16)

    @pl.when(i == 0)
    def _():
        small_ref[GLA_RANK + FOX_HEADS:, :] = jnp.zeros((SMALL_W - GLA_RANK - FOX_HEADS, D_MODEL), F32)

    for row0, n, dst in ((WIN_GA0, GLA_RANK, 0), (WIN_FF0, FOX_HEADS, SMALL_FF0)):
        step, off = _tail_slot(row0)

        @pl.when(i == step)
        def _(n=n, dst=dst, off=off):
            small_ref[dst:dst + n, :] = b_ref[off:off + n, :]


def _prep_in_weights(w_in):
    w_t = jnp.swapaxes(w_in, 1, 2)
    rows, tail = WPREP_ROWS, WPREP_TAIL
    for row0, n in ((WIN_GA0, GLA_RANK), (WIN_FF0, FOX_HEADS)):
        assert 0 <= _tail_slot(row0)[1] and _tail_slot(row0)[1] + n <= tail
    nbytes = 2 * (rows + tail) * D_MODEL * 4 + 2 * rows * D_MODEL * 2 + 2 * rows * D_MODEL * 4 + 2 * SMALL_W * D_MODEL * 4
    main, small_t = pl.pallas_call(
        _wprep_kernel,
        grid=(DEPTH, N_PROJ // rows),
        in_specs=[
            pl.BlockSpec((None, rows, D_MODEL), lambda l, i: (l, i, 0)),
            pl.BlockSpec((None, tail, D_MODEL), lambda l, i: (l, (i + 1) * (rows // tail), 0)),
        ],
        out_specs=[
            pl.BlockSpec((None, D_MODEL, rows), lambda l, i: (l, 0, i)),
            pl.BlockSpec((None, SMALL_W, D_MODEL), lambda l, i: (l, 0, 0)),
        ],
        out_shape=[
            jax.ShapeDtypeStruct((DEPTH, D_MODEL, N_PROJ), BF16),
            jax.ShapeDtypeStruct((DEPTH, SMALL_W, D_MODEL), F32),
        ],
        compiler_params=_params(("arbitrary", "arbitrary"), nbytes),
        name="wprep",
    )(w_t, w_t)
    return main, jnp.swapaxes(small_t, 1, 2).astype(BF16)


def kernel(x, meta_tokens, norm_mix_g, w_in, b_forget, w_alpha2, b_alpha, gla_norm_g, w_o_fox, w_o_gla,
           w_out, norm_mlp_g, w_ff1, w_ff2, final_norm_g):
    hx = x.astype(F32).reshape(MX, D_MODEL)
    hm = jnp.pad(meta_tokens.astype(F32), ((0, MM - N_META), (0, 0)))

    w_main, w_small = _prep_in_weights(w_in)
    wa_pad = jnp.pad(w_alpha2, ((0, 0), (0, SMALL_W - GLA_RANK), (0, 0))).astype(BF16)
    norm_mix_g3 = norm_mix_g.reshape(DEPTH, 1, D_MODEL)
    norm_mlp_g3 = norm_mlp_g.reshape(DEPTH, 1, D_MODEL)
    b_alpha3 = b_alpha.reshape(DEPTH, 1, GLA_KEY_WIDTH)
    gla_norm_g3 = gla_norm_g.reshape(DEPTH, 1, GLA_VAL_WIDTH)
    final_g = final_norm_g.reshape(1, D_MODEL)

    for l in range(DEPTH):
        px, smx, ffx, pm, smm, ffm = _inproj(l, hx, hm, norm_mix_g3, w_main, w_small)
        negc = _fox_cumsum(l, b_forget, ffx.reshape(FOX_HEADS, BATCH, SEQ_LANE_ROWS, LANES), ffm)
        ofx, ofm = _fox_attention(px, pm, negc)
        ogx, ogm, w_o_fox_b, w_o_gla_b, w_out_b, w_ff1_b, w_ff2_b = _gla(
            l, px, smx, pm, smm, wa_pad, b_alpha3, gla_norm_g3, (w_o_fox, w_o_gla, w_out, w_ff1, w_ff2))
        yx, ym = _merge(l, ofx, ogx, px, ofm, ogm, pm, w_o_fox_b, w_o_gla_b)
        hx, hm = _outproj(l, yx, hx, ym, hm, w_out_b)
        if l < DEPTH - 1:
            hx, hm = _mlp(l, hx, hm, norm_mlp_g3, w_ff1_b, w_ff2_b)
        else:
            hx = _mlp_final(l, hx, norm_mlp_g3, w_ff1_b, w_ff2_b, final_g)

    return hx.reshape(BATCH, SEQ, D_MODEL)
```
